```python
import jax, jax.numpy as jnp
from jax import lax
import numpy as np

D_MODEL = 1024
BATCH = 4
SEQ = 4096
DEPTH = 2

HEAD_DIM = 64
BLOCK = 128
ROPE_THETA = 10000.0
EPS = 1e-6
A_HEADS = 8
IDX_HEADS = 4
IDX_DIM = 64
TOPK_MAX = 256
B_Q_HEADS = 8
B_KV_HEADS = 2
WINDOW = 128
C_HEADS = 8

A_WIDTH = A_HEADS * HEAD_DIM
B_WIDTH = B_Q_HEADS * HEAD_DIM
B_KV_WIDTH = B_KV_HEADS * HEAD_DIM
C_WIDTH = C_HEADS * HEAD_DIM

SPLIT_SIZES = (
    A_WIDTH, A_WIDTH, A_WIDTH, A_WIDTH,
    IDX_HEADS * IDX_DIM, IDX_DIM, IDX_HEADS,
    B_WIDTH, B_KV_WIDTH, B_KV_WIDTH, B_WIDTH,
    C_WIDTH, C_WIDTH, C_WIDTH, C_WIDTH,
    D_MODEL, D_MODEL, D_MODEL,
)
D_IN = int(sum(SPLIT_SIZES))
SPLIT_OFFSETS = [int(o) for o in np.cumsum(SPLIT_SIZES)[:-1]]

kernel_name = "hybrid_dsa_swa_stickbreak_gated_trunk"


def rms_norm(x, g):
    xf = x.astype(jnp.float32)
    y = xf * lax.rsqrt(jnp.mean(xf * xf, axis=-1, keepdims=True) + EPS)
    return (y * g.astype(jnp.float32)).astype(x.dtype)


def rope_tables(seq, dim):
    inv = 1.0 / (ROPE_THETA ** (jnp.arange(0, dim, 2, dtype=jnp.float32) / dim))
    ang = jnp.arange(seq, dtype=jnp.float32)[:, None] * inv[None, :]
    return jnp.cos(ang), jnp.sin(ang)


def apply_rope(x, cos, sin):
    x1, x2 = jnp.split(x.astype(jnp.float32), 2, axis=-1)
    c = cos[None, :, None, :]
    s = sin[None, :, None, :]
    return jnp.concatenate([x1 * c - x2 * s, x1 * s + x2 * c], axis=-1).astype(x.dtype)


def dsa_attention(q, k, v, iq, ik, iw):
    bsz, seq, n_heads, dh = q.shape
    topk = min(TOPK_MAX, seq // 4)
    n_blocks = seq // BLOCK
    key_pos = jnp.arange(seq)
    w_scale = (IDX_HEADS ** -0.5) * (IDX_DIM ** -0.5)
    ikf = ik.astype(jnp.float32)

    def block(i):
        t0 = i * BLOCK
        qb = lax.dynamic_slice_in_dim(q, t0, BLOCK, axis=1)
        iqb = lax.dynamic_slice_in_dim(iq, t0, BLOCK, axis=1).astype(jnp.float32)
        iwb = lax.dynamic_slice_in_dim(iw, t0, BLOCK, axis=1).astype(jnp.float32)
        qpos = t0 + jnp.arange(BLOCK)
        causal = key_pos[None, :] <= qpos[:, None]
        dots = jnp.einsum('bthd,bsd->bths', iqb, ikf)
        score = jnp.einsum('bth,bths->bts', iwb * w_scale, jax.nn.relu(dots))
        score = jnp.where(causal[None], score, -jnp.inf)
        top_val, top_idx = lax.top_k(score, topk)
        valid = jnp.isfinite(top_val)
        k_sel = jax.vmap(lambda kk, ii: kk[ii])(k, top_idx)
        v_sel = jax.vmap(lambda vv, ii: vv[ii])(v, top_idx)
        logits = jnp.einsum('bthd,btkhd->bhtk', qb, k_sel).astype(jnp.float32) * (dh ** -0.5)
        logits = jnp.where(valid[:, None], logits, -jnp.inf)
        p = jax.nn.softmax(logits, axis=-1)
        return jnp.einsum('bhtk,btkhd->bthd', p.astype(v.dtype), v_sel)

    out = lax.map(block, jnp.arange(n_blocks))
    return out.transpose(1, 0, 2, 3, 4).reshape(bsz, seq, n_heads, dh)


def swa_sinks_attention(q, k, v, sinks):
    bsz, seq, hq, dh = q.shape
    hkv = k.shape[2]
    grp = hq // hkv
    nb = seq // BLOCK
    qb = q.reshape(bsz, nb, BLOCK, hkv, grp, dh)

    def band(t):
        tb = t.reshape(bsz, nb, BLOCK, hkv, dh)
        prev = jnp.concatenate([jnp.zeros_like(tb[:, :1]), tb[:, :-1]], axis=1)
        return jnp.concatenate([prev, tb], axis=2)

    kb, vb = band(k), band(v)
    logits = jnp.einsum('bnqhgd,bnkhd->bnhgqk', qb, kb).astype(jnp.float32) * (dh ** -0.5)
    blk = jnp.arange(nb)[:, None, None]
    qpos = blk * BLOCK + jnp.arange(BLOCK)[None, :, None]
    kpos = blk * BLOCK - BLOCK + jnp.arange(2 * BLOCK)[None, None, :]
    allowed = (kpos <= qpos) & (kpos > qpos - WINDOW) & (kpos >= 0)
    logits = jnp.where(allowed[None, :, None, None], logits, -jnp.inf)
    sink = jnp.broadcast_to(
        sinks.astype(jnp.float32).reshape(hkv, grp)[None, None, :, :, None, None],
        logits.shape[:-1] + (1,))
    p = jax.nn.softmax(jnp.concatenate([logits, sink], axis=-1), axis=-1)[..., :-1]
    out = jnp.einsum('bnhgqk,bnkhd->bnqhgd', p.astype(v.dtype), vb)
    return out.reshape(bsz, seq, hq, dh)


def stick_breaking_attention(q, k, v):
    bsz, seq, n_heads, dh = q.shape
    nb = seq // BLOCK
    key_pos = jnp.arange(seq)

    def block(i):
        t0 = i * BLOCK
        qb = lax.dynamic_slice_in_dim(q, t0, BLOCK, axis=1)
        qpos = t0 + jnp.arange(BLOCK)
        strict = (key_pos[None, :] < qpos[:, None])[None, None]
        z = jnp.einsum('bthd,bshd->bhts', qb, k).astype(jnp.float32) * (dh ** -0.5)
        log_beta = jax.nn.log_sigmoid(z)
        log_1m = jnp.where(strict, jax.nn.log_sigmoid(-z), 0.0)
        after = lax.cumsum(log_1m, axis=3, reverse=True) - log_1m
        a = jnp.where(strict, jnp.exp(log_beta + after), 0.0)
        return jnp.einsum('bhts,bshd->bthd', a.astype(v.dtype), v)

    out = lax.map(block, jnp.arange(nb))
    return out.transpose(1, 0, 2, 3, 4).reshape(bsz, seq, n_heads, dh)


def hybrid_layer(x, c_silu, cos, sin, norm_g, w_ada, b_ada, w_in, sinks, w_br_a, w_br_b, w_br_c, w_out):
    bsz, seq, _ = x.shape
    mod = c_silu @ w_ada + b_ada
    shift, scale, gate = jnp.split(mod, 3, axis=-1)
    u = rms_norm(x, norm_g) * (1 + scale[:, None]) + shift[:, None]
    z = u @ w_in
    (aq, ak, av, ag, iq, ik, iw, bq, bk, bv, bg, cq, ck, cv, cg, ma, mb, mc) = jnp.split(z, SPLIT_OFFSETS, axis=-1)

    def heads(t, n):
        return t.reshape(bsz, seq, n, -1)

    ya = dsa_attention(
        apply_rope(heads(aq, A_HEADS), cos, sin),
        apply_rope(heads(ak, A_HEADS), cos, sin),
        heads(av, A_HEADS),
        apply_rope(heads(iq, IDX_HEADS), cos, sin),
        apply_rope(ik[:, :, None, :], cos, sin)[:, :, 0, :],
        iw)
    ya = ya.reshape(bsz, seq, A_WIDTH) * jax.nn.silu(ag)
    yb = swa_sinks_attention(
        apply_rope(heads(bq, B_Q_HEADS), cos, sin),
        apply_rope(heads(bk, B_KV_HEADS), cos, sin),
        heads(bv, B_KV_HEADS),
        sinks)
    yb = yb.reshape(bsz, seq, B_WIDTH) * jax.nn.silu(bg)
    yc = stick_breaking_attention(heads(cq, C_HEADS), heads(ck, C_HEADS), heads(cv, C_HEADS))
    yc = yc.reshape(bsz, seq, C_WIDTH) * jax.nn.silu(cg)

    merged = (jax.nn.sigmoid(ma) * (ya @ w_br_a)
              + jax.nn.sigmoid(mb) * (yb @ w_br_b)
              + jax.nn.sigmoid(mc) * (yc @ w_br_c))
    return x + gate[:, None] * (merged @ w_out)


def setup_inputs(seed: int = 0) -> dict:
    key = jax.random.key(seed)
    ks = jax.random.split(key, 13)
    f32 = jnp.float32

    def nrm(k, shape, scale):
        return jax.random.normal(k, shape, dtype=f32) * scale

    return {
        "x": nrm(ks[0], (BATCH, SEQ, D_MODEL), 1.0),
        "c": nrm(ks[1], (BATCH, D_MODEL), 1.0),
        "norm_g": 1.0 + nrm(ks[2], (DEPTH, D_MODEL), 0.02),
        "w_ada": nrm(ks[3], (DEPTH, D_MODEL, 3 * D_MODEL), 0.2 * D_MODEL ** -0.5),
        "b_ada": nrm(ks[4], (DEPTH, 3 * D_MODEL), 0.02),
        "w_in": nrm(ks[5], (DEPTH, D_MODEL, D_IN), D_MODEL ** -0.5),
        "sinks": nrm(ks[6], (DEPTH, B_Q_HEADS), 0.5),
        "w_br_a": nrm(ks[7], (DEPTH, A_WIDTH, D_MODEL), A_WIDTH ** -0.5),
        "w_br_b": nrm(ks[8], (DEPTH, B_WIDTH, D_MODEL), B_WIDTH ** -0.5),
        "w_br_c": nrm(ks[9], (DEPTH, C_WIDTH, D_MODEL), C_WIDTH ** -0.5),
        "w_out": nrm(ks[10], (DEPTH, D_MODEL, D_MODEL), D_MODEL ** -0.5),
        "final_g": 1.0 + nrm(ks[11], (D_MODEL,), 0.02),
    }


def reference(x, c, norm_g, w_ada, b_ada, w_in, sinks, w_br_a, w_br_b, w_br_c, w_out, final_g):
    seq = x.shape[1]
    cos, sin = rope_tables(seq, HEAD_DIM)
    c_silu = jax.nn.silu(c)
    h = x
    for l in range(DEPTH):
        h = hybrid_layer(h, c_silu, cos, sin, norm_g[l], w_ada[l], b_ada[l], w_in[l], sinks[l],
                         w_br_a[l], w_br_b[l], w_br_c[l], w_out[l])
    return rms_norm(h, final_g)
```

```python
import functools

import jax
import jax.numpy as jnp
import numpy as np
from jax import lax
from jax.experimental import pallas as pl
from jax.experimental.pallas import tpu as pltpu

F32 = jnp.float32
BF16 = jnp.bfloat16
I32 = jnp.int32

D_MODEL = 1024
HEAD_DIM = 64
ROPE_THETA = 10000.0
EPS = 1e-6
N_HEADS = 8
N_PAIRS = N_HEADS // 2
IDX_HEADS = 4
TOPK_MAX = 256
B_KV_HEADS = 2
WINDOW = 128
MIX_W = N_HEADS * HEAD_DIM

LANES = 128
VMEM_LIMIT = 48 * 1024 * 1024
NEG = -1e30
INT_MIN = -(2 ** 31)

C_MA, C_MB, C_MC = 0, 1024, 2048
C_AV, C_AG, C_BG, C_CQ, C_CK, C_CV, C_CG = 3072, 3584, 4096, 4608, 5120, 5632, 6144
C_AQ, C_AK, C_BQ = 6656, 7168, 7680
C_IQ = 8192
C_IK2, C_BK2, C_BK2S, C_BV2, C_BV2S, C_IW = 8448, 8576, 8704, 8832, 8960, 9088
NP = 9216
TN = 256
FIRST_ROPE_TILE = C_AQ // TN
HALF_ROPE_TILE = C_BK2S // TN
PLAIN_TAIL_TILE = C_BV2S // TN

_SPLIT_NAMES = ("aq", "ak", "av", "ag", "iq", "ik", "iw", "bq", "bk", "bv", "bg",
                "cq", "ck", "cv", "cg", "ma", "mb", "mc")
_SPLIT_SIZES = (512, 512, 512, 512, 256, 64, 4, 512, 128, 128, 512,
                512, 512, 512, 512, 1024, 1024, 1024)


def _cparams(n_grid):
    return pltpu.CompilerParams(dimension_semantics=("arbitrary",) * n_grid,
                                vmem_limit_bytes=VMEM_LIMIT)


def _sigmoid(x):
    return 1.0 / (1.0 + jnp.exp(-x))


def _dot_nt(a, b):
    return lax.dot_general(a, b, (((1,), (1,)), ((), ())), preferred_element_type=F32)


def _lo_half(rows):
    return lax.broadcasted_iota(I32, (rows, LANES), 1) < HEAD_DIM


def _mod_kernel(c_ref, w_ref, b_ref, o_ref):
    c = c_ref[...]
    cs = c * _sigmoid(c)
    o_ref[0] = jnp.dot(cs, w_ref[0], preferred_element_type=F32,
                       precision=lax.Precision.HIGHEST) + b_ref[0]


def _mod_call(c_pad, w_ada, b_ada):
    depth, d, d3 = w_ada.shape
    rows = c_pad.shape[0]
    return pl.pallas_call(
        _mod_kernel,
        out_shape=jax.ShapeDtypeStruct((depth, rows, d3), F32),
        grid=(depth, d3 // d),
        in_specs=[
            pl.BlockSpec((rows, d), lambda l, j: (0, 0)),
            pl.BlockSpec((1, d, d), lambda l, j: (l, 0, j)),
            pl.BlockSpec((1, 1, d), lambda l, j: (l, 0, j)),
        ],
        out_specs=pl.BlockSpec((1, rows, d), lambda l, j: (l, 0, j)),
        compiler_params=_cparams(2),
        name="adaln_mod",
    )(c_pad, w_ada, b_ada.reshape(depth, 1, d3))


def _inproj_kernel(x_ref, shift_ref, scale_ref, g_ref, w_ref, tab_ref, o_ref, u_sc):
    j = pl.program_id(1)

    @pl.when(j == 0)
    def _():
        x = x_ref[...]
        y = x * lax.rsqrt(jnp.mean(x * x, axis=-1, keepdims=True) + EPS) * g_ref[...]
        u_sc[...] = (y * (1.0 + scale_ref[0]) + shift_ref[0]).astype(BF16)

    acc = jnp.dot(u_sc[...], w_ref[...], preferred_element_type=F32)

    @pl.when(j < FIRST_ROPE_TILE)
    def _():
        o_ref[...] = acc.astype(BF16)

    @pl.when(j >= FIRST_ROPE_TILE)
    def _():
        tm, tn = acc.shape
        lane = lax.broadcasted_iota(I32, (tm, tn), 1)
        first = (lane & (HEAD_DIM - 1)) < HEAD_DIM // 2
        partner = jnp.where(first, pltpu.roll(acc, tn - HEAD_DIM // 2, 1),
                            pltpu.roll(acc, HEAD_DIM // 2, 1))
        o_ref[...] = (acc * tab_ref[0, 0] + partner * tab_ref[0, 1]).astype(BF16)


def _table_id(j):
    return jnp.where(j == HALF_ROPE_TILE, 1, jnp.where(j == PLAIN_TAIL_TILE, 2, 0))


def _inproj_call(h, mod3, norm_g, w, tabs, seq, tm):
    rows, d = h.shape
    nqb = seq // tm
    return pl.pallas_call(
        _inproj_kernel,
        out_shape=jax.ShapeDtypeStruct((rows, NP), BF16),
        grid=(rows // tm, NP // TN),
        in_specs=[
            pl.BlockSpec((tm, d), lambda i, j: (i, 0)),
            pl.BlockSpec((1, 1, d), lambda i, j: (i // nqb, 0, 0)),
            pl.BlockSpec((1, 1, d), lambda i, j: (i // nqb, 0, 1)),
            pl.BlockSpec((1, d), lambda i, j: (0, 0)),
            pl.BlockSpec((d, TN), lambda i, j: (0, j)),
            pl.BlockSpec((1, 2, tm, TN), lambda i, j: (_table_id(j), 0, i % nqb, 0)),
        ],
        out_specs=pl.BlockSpec((tm, TN), lambda i, j: (i, j)),
        scratch_shapes=[pltpu.VMEM((tm, d), BF16)],
        compiler_params=_cparams(2),
        name="norm_inproj_rope",
    )(h, mod3, mod3, norm_g.reshape(1, d), w, tabs)


def _rope_tables(seq):
    half = HEAD_DIM // 2
    inv = 1.0 / (ROPE_THETA ** (jnp.arange(0, HEAD_DIM, 2, dtype=F32) / HEAD_DIM))
    ang = jnp.arange(seq, dtype=F32)[:, None] * inv[None, :]
    cos, sin = jnp.cos(ang), jnp.sin(ang)
    lane = np.arange(TN)
    sign = np.where((lane % HEAD_DIM) < half, -1.0, 1.0).astype(np.float32)
    cos_t = cos[:, lane % half]
    sin_t = sin[:, lane % half] * sign[None, :]
    plain = (lane >= LANES)[None, :]
    full = jnp.stack([cos_t, sin_t])
    halfp = jnp.stack([jnp.where(plain, 1.0, cos_t), jnp.where(plain, 0.0, sin_t)])
    ident = jnp.stack([jnp.ones_like(cos_t), jnp.zeros_like(sin_t)])
    return jnp.stack([full, halfp, ident])


def _prep_w_in(w):
    parts, off = {}, 0
    for name, size in zip(_SPLIT_NAMES, _SPLIT_SIZES):
        parts[name] = w[:, off:off + size]
        off += size
    k0, k1 = parts["bk"][:, :HEAD_DIM], parts["bk"][:, HEAD_DIM:]
    v0, v1 = parts["bv"][:, :HEAD_DIM], parts["bv"][:, HEAD_DIM:]
    iw_pad = jnp.zeros((w.shape[0], LANES - parts["iw"].shape[1]), w.dtype)
    cols = [parts["ma"], parts["mb"], parts["mc"],
            parts["av"], parts["ag"], parts["bg"], parts["cq"], parts["ck"], parts["cv"], parts["cg"],
            parts["aq"], parts["ak"], parts["bq"], parts["iq"],
            parts["ik"], parts["ik"], k0, k1, k1, k0, v0, v1, v1, v0, parts["iw"], iw_pad]
    out = jnp.concatenate(cols, axis=1).astype(BF16)
    assert out.shape[1] == NP
    return out


def _gated_store(o_ref, g_ref, p, val):
    sl = slice(p * LANES, (p + 1) * LANES)
    g = g_ref[:, sl].astype(F32)
    o_ref[:, sl] = (val * (g * _sigmoid(g))).astype(o_ref.dtype)


def _dsa_kernel(q_ref, k_ref, v_ref, g_ref, iq_ref, ik_ref, iw_ref, o_ref,
                keys_sc, qm_sc, iqm_sc, w_sc, m_sc, l_sc, acc_sc, *, tq, ck, topk):
    i = pl.program_id(1)
    nck = (i + 1) * (tq // ck)
    row0 = i * tq
    rep = ck // LANES
    lo_half = _lo_half(tq)
    kf = jnp.float32(topk)

    for p in range(N_PAIRS):
        qp = q_ref[:, p * LANES:(p + 1) * LANES].astype(F32) * (HEAD_DIM ** -0.5)
        qm_sc[2 * p] = jnp.where(lo_half, qp, 0.0).astype(BF16)
        qm_sc[2 * p + 1] = jnp.where(lo_half, 0.0, qp).astype(BF16)
    for p in range(IDX_HEADS // 2):
        t = iq_ref[:, p * LANES:(p + 1) * LANES].astype(F32)
        iqm_sc[2 * p] = jnp.where(lo_half, t, 0.0).astype(BF16)
        iqm_sc[2 * p + 1] = jnp.where(lo_half, 0.0, t).astype(BF16)
    iw = iw_ref[...].astype(F32) * ((IDX_HEADS ** -0.5) * (HEAD_DIM ** -0.5))
    for h in range(IDX_HEADS):
        w_sc[h] = jnp.broadcast_to(iw[:, h:h + 1], (tq, LANES))

    row = row0 + lax.broadcasted_iota(I32, (tq, ck), 0)
    col_l = lax.broadcasted_iota(I32, (tq, ck), 1)

    def score_chunk(c, carry):
        k0 = pl.multiple_of(c * ck, ck)
        ikc = ik_ref[pl.ds(k0, ck), :]
        sc = jnp.zeros((tq, ck), F32)
        for h in range(IDX_HEADS):
            d = _dot_nt(iqm_sc[h], ikc)
            sc = sc + jnp.tile(w_sc[h], (1, rep)) * jnp.maximum(d, 0.0)
        bits = pltpu.bitcast(sc, I32)
        key = bits ^ ((bits >> 31) & 0x7FFFFFFF)
        keys_sc[c] = jnp.where(k0 + col_l <= row, key, INT_MIN)
        return carry

    lax.fori_loop(0, nck, score_chunk, 0)

    def count_ge(thr):
        thr_w = jnp.tile(thr, (1, rep))

        def body(c, part):
            ones = jnp.where(keys_sc[c] >= thr_w, 1.0, 0.0)
            f = ones[:, :LANES]
            for t in range(1, rep):
                f = f + ones[:, t * LANES:(t + 1) * LANES]
            return part + f

        part = lax.fori_loop(0, nck, body, jnp.zeros((tq, LANES), F32))
        return jnp.sum(part, axis=1, keepdims=True)

    def bisect(it, res):
        cand = res + jnp.left_shift(jnp.int32(1), 31 - it)
        return jnp.where(count_ge(cand) >= kf, cand, res)

    res = lax.fori_loop(0, 32, bisect, jnp.full((tq, LANES), INT_MIN, I32))
    tau = jnp.maximum(res, INT_MIN + 1)
    n_ge = count_ge(tau)

    @pl.when(jnp.max(n_ge) > kf)
    def _():
        need = kf - count_ge(tau + 1)
        tau_w = jnp.tile(tau, (1, rep))
        incl = (lax.broadcasted_iota(I32, (ck, ck), 0)
                <= lax.broadcasted_iota(I32, (ck, ck), 1)).astype(F32).astype(BF16)

        def body(c, seen):
            kc = keys_sc[c]
            eq = jnp.where(kc == tau_w, 1.0, 0.0)
            rank = jnp.dot(eq.astype(BF16), incl, preferred_element_type=F32) + jnp.tile(seen, (1, rep))
            drop = eq * jnp.where(rank > need, 1.0, 0.0)
            keys_sc[c] = jnp.where(drop > 0.5, tau_w - 1, kc)
            return seen + jnp.sum(eq, axis=1, keepdims=True)

        lax.fori_loop(0, nck, body, jnp.zeros((tq, LANES), F32))

    m_sc[...] = jnp.full(m_sc.shape, NEG, F32)
    l_sc[...] = jnp.zeros(l_sc.shape, F32)
    acc_sc[...] = jnp.zeros(acc_sc.shape, F32)
    tau_w = jnp.tile(tau, (1, rep))
    ones_v = jnp.ones((ck, LANES), BF16)

    def attend(c, carry):
        k0 = pl.multiple_of(c * ck, ck)
        bias = jnp.where(keys_sc[c] >= tau_w, 0.0, NEG)
        for p in range(N_PAIRS):
            sl = slice(p * LANES, (p + 1) * LANES)
            kp = k_ref[pl.ds(k0, ck), sl]
            vp = jnp.concatenate([v_ref[pl.ds(k0, ck), sl], ones_v], axis=1)
            for e in range(2):
                h = 2 * p + e
                s = _dot_nt(qm_sc[h], kp) + bias
                m_prev = m_sc[h]
                m_next = jnp.maximum(m_prev, jnp.max(s, axis=1, keepdims=True))
                pe = jnp.exp(s - jnp.tile(m_next, (1, rep)))
                alpha = jnp.exp(m_prev - m_next)
                pv = jnp.dot(pe.astype(BF16), vp, preferred_element_type=F32)
                l_sc[h] = alpha * l_sc[h] + pv[:, LANES:]
                acc_sc[h] = alpha * acc_sc[h] + pv[:, :LANES]
                m_sc[h] = m_next
        return carry

    lax.fori_loop(0, nck, attend, 0)

    for p in range(N_PAIRS):
        out = jnp.where(lo_half, acc_sc[2 * p] / l_sc[2 * p], acc_sc[2 * p + 1] / l_sc[2 * p + 1])
        _gated_store(o_ref, g_ref, p, out)


def _dsa_call(z, batch, seq, tq, ck):
    topk = min(TOPK_MAX, seq // 4)
    nq = seq // tq
    kern = functools.partial(_dsa_kernel, tq=tq, ck=ck, topk=topk)
    return pl.pallas_call(
        kern,
        out_shape=jax.ShapeDtypeStruct((batch * seq, MIX_W), BF16),
        grid=(batch, nq),
        in_specs=[
            pl.BlockSpec((tq, MIX_W), lambda b, i: (b * nq + i, C_AQ // MIX_W)),
            pl.BlockSpec((seq, MIX_W), lambda b, i: (b, C_AK // MIX_W)),
            pl.BlockSpec((seq, MIX_W), lambda b, i: (b, C_AV // MIX_W)),
            pl.BlockSpec((tq, MIX_W), lambda b, i: (b * nq + i, C_AG // MIX_W)),
            pl.BlockSpec((tq, 2 * LANES), lambda b, i: (b * nq + i, C_IQ // (2 * LANES))),
            pl.BlockSpec((seq, LANES), lambda b, i: (b, C_IK2 // LANES)),
            pl.BlockSpec((tq, LANES), lambda b, i: (b * nq + i, C_IW // LANES)),
        ],
        out_specs=pl.BlockSpec((tq, MIX_W), lambda b, i: (b * nq + i, 0)),
        scratch_shapes=[
            pltpu.VMEM((seq // ck, tq, ck), I32),
            pltpu.VMEM((N_HEADS, tq, LANES), BF16),
            pltpu.VMEM((IDX_HEADS, tq, LANES), BF16),
            pltpu.VMEM((IDX_HEADS, tq, LANES), F32),
            pltpu.VMEM((N_HEADS, tq, LANES), F32),
            pltpu.VMEM((N_HEADS, tq, LANES), F32),
            pltpu.VMEM((N_HEADS, tq, LANES), F32),
        ],
        compiler_params=_cparams(2),
        name="dsa_attention",
    )(z, z, z, z, z, z, z)


def _swa_kernel(sinks_ref, q_ref, k2_ref, k2s_ref, v2_ref, v2s_ref, g_ref, o_ref, *, tq):
    i = pl.program_id(1)
    span = tq + WINDOW
    kstart = pl.multiple_of(jnp.maximum(i * tq - WINDOW, 0), WINDOW)
    lo_half = _lo_half(tq)
    row = i * tq + lax.broadcasted_iota(I32, (tq, span), 0)
    col = kstart + lax.broadcasted_iota(I32, (tq, span), 1)
    bias = jnp.where((col <= row) & (col > row - WINDOW), 0.0, NEG)
    kv = {False: (k2s_ref[pl.ds(kstart, span), :], v2s_ref[pl.ds(kstart, span), :]),
          True: (k2_ref[pl.ds(kstart, span), :], v2_ref[pl.ds(kstart, span), :])}
    grp = N_HEADS // B_KV_HEADS
    for p in range(N_PAIRS):
        qp = q_ref[:, p * LANES:(p + 1) * LANES].astype(F32) * (HEAD_DIM ** -0.5)
        halves = []
        for e in range(2):
            h = 2 * p + e
            qm = jnp.where(lo_half if e == 0 else ~lo_half, qp, 0.0).astype(BF16)
            kk, vv = kv[(h // grp) == e]
            s = _dot_nt(qm, kk) + bias
            sink = sinks_ref[h]
            m = jnp.maximum(jnp.max(s, axis=1, keepdims=True), sink)
            pe = jnp.exp(s - m)
            den = jnp.sum(pe, axis=1, keepdims=True) + jnp.exp(sink - m)
            halves.append(jnp.dot(pe.astype(BF16), vv, preferred_element_type=F32) / den)
        _gated_store(o_ref, g_ref, p, jnp.where(lo_half, halves[0], halves[1]))


def _swa_call(z, sinks, batch, seq, tq):
    nq = seq // tq
    kv_spec = lambda c: pl.BlockSpec((seq, LANES), lambda b, i: (b, c // LANES))
    return pl.pallas_call(
        functools.partial(_swa_kernel, tq=tq),
        out_shape=jax.ShapeDtypeStruct((batch * seq, MIX_W), BF16),
        grid=(batch, nq),
        in_specs=[
            pl.BlockSpec(memory_space=pltpu.SMEM),
            pl.BlockSpec((tq, MIX_W), lambda b, i: (b * nq + i, C_BQ // MIX_W)),
            kv_spec(C_BK2), kv_spec(C_BK2S), kv_spec(C_BV2), kv_spec(C_BV2S),
            pl.BlockSpec((tq, MIX_W), lambda b, i: (b * nq + i, C_BG // MIX_W)),
        ],
        out_specs=pl.BlockSpec((tq, MIX_W), lambda b, i: (b * nq + i, 0)),
        compiler_params=_cparams(2),
        name="swa_sinks_attention",
    )(sinks, z, z, z, z, z, z)


def _sb_kernel(q_ref, k_ref, v_ref, g_ref, o_ref, *, tq, ck):
    i = pl.program_id(1)
    nck = (i + 1) * (tq // ck)
    rep = ck // LANES
    lo_half = _lo_half(tq)
    row = i * tq + lax.broadcasted_iota(I32, (tq, ck), 0)
    col_l = lax.broadcasted_iota(I32, (tq, ck), 1)
    later = (lax.broadcasted_iota(I32, (ck, ck), 0)
             > lax.broadcasted_iota(I32, (ck, ck), 1)).astype(F32).astype(BF16)

    for p in range(N_PAIRS):
        sl = slice(p * LANES, (p + 1) * LANES)
        qp = q_ref[:, sl].astype(F32) * (HEAD_DIM ** -0.5)
        halves = []
        for e in range(2):
            qm = jnp.where(lo_half if e == 0 else ~lo_half, qp, 0.0).astype(BF16)

            def body(t, carry, qm=qm, sl=sl):
                run, acc = carry
                k0 = pl.multiple_of((nck - 1 - t) * ck, ck)
                z = _dot_nt(qm, k_ref[pl.ds(k0, ck), sl])
                lb = jnp.minimum(z, 0.0) - jnp.log1p(jnp.exp(-jnp.abs(z)))
                strict = k0 + col_l < row
                lm = jnp.where(strict, lb - z, 0.0)
                hi = lm.astype(BF16)
                lo = (lm - hi.astype(F32)).astype(BF16)
                after = (jnp.dot(hi, later, preferred_element_type=F32)
                         + jnp.dot(lo, later, preferred_element_type=F32))
                a = jnp.where(strict, jnp.exp(lb + after + jnp.tile(run, (1, rep))), 0.0)
                acc = acc + jnp.dot(a.astype(BF16), v_ref[pl.ds(k0, ck), sl],
                                    preferred_element_type=F32)
                run = run + jnp.sum(lm, axis=1, keepdims=True)
                return run, acc

            zero = jnp.zeros((tq, LANES), F32)
            _, acc = lax.fori_loop(0, nck, body, (zero, zero))
            halves.append(acc)
        _gated_store(o_ref, g_ref, p, jnp.where(lo_half, halves[0], halves[1]))


def _sb_call(z, batch, seq, tq, ck):
    nq = seq // tq
    return pl.pallas_call(
        functools.partial(_sb_kernel, tq=tq, ck=ck),
        out_shape=jax.ShapeDtypeStruct((batch * seq, MIX_W), BF16),
        grid=(batch, nq),
        in_specs=[
            pl.BlockSpec((tq, MIX_W), lambda b, i: (b * nq + i, C_CQ // MIX_W)),
            pl.BlockSpec((seq, MIX_W), lambda b, i: (b, C_CK // MIX_W)),
            pl.BlockSpec((seq, MIX_W), lambda b, i: (b, C_CV // MIX_W)),
            pl.BlockSpec((tq, MIX_W), lambda b, i: (b * nq + i, C_CG // MIX_W)),
        ],
        out_specs=pl.BlockSpec((tq, MIX_W), lambda b, i: (b * nq + i, 0)),
        compiler_params=_cparams(2),
        name="stick_breaking_attention",
    )(z, z, z, z)


def _merge_kernel(ya_ref, yb_ref, yc_ref, ma_ref, mb_ref, mc_ref, x_ref, gate_ref,
                  wa_ref, wb_ref, wc_ref, wo_ref, fg_ref, o_ref, *, final):
    def branch(y_ref, m_ref, w_ref):
        pr = jnp.dot(y_ref[...], w_ref[...], preferred_element_type=F32)
        return _sigmoid(m_ref[...].astype(F32)) * pr

    merged = (branch(ya_ref, ma_ref, wa_ref) + branch(yb_ref, mb_ref, wb_ref)
              + branch(yc_ref, mc_ref, wc_ref))
    out = x_ref[...] + gate_ref[0] * jnp.dot(merged.astype(BF16), wo_ref[...],
                                             preferred_element_type=F32)
    if final:
        out = out * lax.rsqrt(jnp.mean(out * out, axis=-1, keepdims=True) + EPS) * fg_ref[...]
    o_ref[...] = out


def _merge_call(ya, yb, yc, z, h, mod3, wa, wb, wc, wo, fg, seq, tm, final):
    rows, d = h.shape
    nqb = seq // tm
    y_spec = pl.BlockSpec((tm, MIX_W), lambda i: (i, 0))
    m_spec = lambda c: pl.BlockSpec((tm, d), lambda i: (i, c // d))
    w_spec = pl.BlockSpec((MIX_W, d), lambda i: (0, 0))
    return pl.pallas_call(
        functools.partial(_merge_kernel, final=final),
        out_shape=jax.ShapeDtypeStruct((rows, d), F32),
        grid=(rows // tm,),
        in_specs=[
            y_spec, y_spec, y_spec, m_spec(C_MA), m_spec(C_MB), m_spec(C_MC),
            pl.BlockSpec((tm, d), lambda i: (i, 0)),
            pl.BlockSpec((1, 1, d), lambda i: (i // nqb, 0, 2)),
            w_spec, w_spec, w_spec,
            pl.BlockSpec((d, d), lambda i: (0, 0)),
            pl.BlockSpec((1, d), lambda i: (0, 0)),
        ],
        out_specs=pl.BlockSpec((tm, d), lambda i: (i, 0)),
        compiler_params=_cparams(1),
        name="merge_outproj_residual",
    )(ya, yb, yc, z, z, z, h, mod3, wa, wb, wc, wo, fg.reshape(1, d))


def _tiles(seq):
    return min(1024, seq), min(256, seq), min(512, seq)


def kernel(x, c, norm_g, w_ada, b_ada, w_in, sinks, w_br_a, w_br_b, w_br_c, w_out, final_g):
    batch, seq, d = x.shape
    depth = w_in.shape[0]
    assert d == D_MODEL and seq % 256 == 0 and seq >= 256 + WINDOW
    tm_in, tq, tm_mg = _tiles(seq)

    c_pad = jnp.zeros((8, d), F32).at[:batch].set(c)
    mod = _mod_call(c_pad, w_ada, b_ada)[:, :batch]
    tabs = _rope_tables(seq)
    h = x.reshape(batch * seq, d)
    for l in range(depth):
        mod3 = mod[l].reshape(batch, 1, 3 * d)
        z = _inproj_call(h, mod3, norm_g[l], _prep_w_in(w_in[l]), tabs, seq, tm_in)
        ya = _dsa_call(z, batch, seq, tq, tq)
        yb = _swa_call(z, sinks[l], batch, seq, tq)
        yc = _sb_call(z, batch, seq, tq, tq)
        h = _merge_call(ya, yb, yc, z, h, mod3, w_br_a[l].astype(BF16), w_br_b[l].astype(BF16),
                        w_br_c[l].astype(BF16), w_out[l].astype(BF16), final_g, seq, tm_mg,
                        final=(l == depth - 1))
    return h.reshape(batch, seq, d)
```

```python
import functools

import jax
import jax.numpy as jnp
import numpy as np
from jax import lax
from jax.experimental import pallas as pl
from jax.experimental.pallas import tpu as pltpu

F32 = jnp.float32
BF16 = jnp.bfloat16
I32 = jnp.int32

D_MODEL = 1024
HEAD_DIM = 64
ROPE_THETA = 10000.0
EPS = 1e-6
N_HEADS = 8
N_PAIRS = N_HEADS // 2
IDX_HEADS = 4
TOPK_MAX = 256
B_KV_HEADS = 2
WINDOW = 128
MIX_W = N_HEADS * HEAD_DIM

LANES = 128
MOD_ROWS = 8
VMEM_LIMIT = 48 * 1024 * 1024
NEG = -1e30
LOG2E = 1.4426950408889634
INT_MIN = -(2 ** 31)

C_MA, C_MB, C_MC = 0, 1024, 2048
C_AV, C_AG, C_BG, C_CQ, C_CK, C_CV, C_CG = 3072, 3584, 4096, 4608, 5120, 5632, 6144
C_AQ, C_AK, C_BQ = 6656, 7168, 7680
C_IQ = 8192
C_IK2, C_BK2, C_BK2S, C_BV2, C_BV2S, C_IW = 8448, 8576, 8704, 8832, 8960, 9088
NP = 9216
TN = 512
FIRST_ROPE_TILE = C_AQ // TN
LAST_TILE = NP // TN - 1
assert C_BK2S == LAST_TILE * TN and C_AQ % TN == 0

_SPLIT_NAMES = ("aq", "ak", "av", "ag", "iq", "ik", "iw", "bq", "bk", "bv", "bg",
                "cq", "ck", "cv", "cg", "ma", "mb", "mc")
_SPLIT_SIZES = (512, 512, 512, 512, 256, 64, 4, 512, 128, 128, 512,
                512, 512, 512, 512, 1024, 1024, 1024)


def _cparams(n_grid):
    return pltpu.CompilerParams(dimension_semantics=("arbitrary",) * n_grid,
                                vmem_limit_bytes=VMEM_LIMIT)


def _sigmoid(x):
    return 1.0 / (1.0 + jnp.exp(-x))


def _dot_nt(a, b):
    return lax.dot_general(a, b, (((1,), (1,)), ((), ())), preferred_element_type=F32)


def _lo_half(rows):
    return lax.broadcasted_iota(I32, (rows, LANES), 1) < HEAD_DIM


def _mod_kernel(c_ref, w_ref, b_ref, o_ref):
    c = c_ref[...]
    cs = c * _sigmoid(c)
    o_ref[0] = jnp.dot(cs, w_ref[0], preferred_element_type=F32,
                       precision=lax.Precision.HIGHEST) + b_ref[0]


def _mod_call(c_pad, w_ada, b_ada):
    depth, d, d3 = w_ada.shape
    rows = c_pad.shape[0]
    return pl.pallas_call(
        _mod_kernel,
        out_shape=jax.ShapeDtypeStruct((depth, rows, d3), F32),
        grid=(depth, d3 // d),
        in_specs=[
            pl.BlockSpec((rows, d), lambda l, j: (0, 0)),
            pl.BlockSpec((1, d, d), lambda l, j: (l, 0, j)),
            pl.BlockSpec((1, 1, d), lambda l, j: (l, 0, j)),
        ],
        out_specs=pl.BlockSpec((1, rows, d), lambda l, j: (l, 0, j)),
        compiler_params=_cparams(2),
        name="adaln_mod",
    )(c_pad, w_ada, b_ada.reshape(depth, 1, d3))


def _inproj_kernel(x_ref, shift_ref, scale_ref, g_ref, w_ref, tab_ref, o_ref, u_sc):
    j = pl.program_id(1)

    @pl.when(j == 0)
    def _():
        x = x_ref[...]
        y = x * lax.rsqrt(jnp.mean(x * x, axis=-1, keepdims=True) + EPS) * g_ref[0]
        u_sc[...] = (y * (1.0 + scale_ref[0]) + shift_ref[0]).astype(BF16)

    acc = jnp.dot(u_sc[...], w_ref[0], preferred_element_type=F32)

    @pl.when(j < FIRST_ROPE_TILE)
    def _():
        o_ref[...] = acc.astype(BF16)

    @pl.when(j >= FIRST_ROPE_TILE)
    def _():
        tm, tn = acc.shape
        lane = lax.broadcasted_iota(I32, (tm, tn), 1)
        first = (lane & (HEAD_DIM - 1)) < HEAD_DIM // 2
        partner = jnp.where(first, pltpu.roll(acc, tn - HEAD_DIM // 2, 1),
                            pltpu.roll(acc, HEAD_DIM // 2, 1))
        roped = (acc * jnp.tile(tab_ref[0], (1, tn // LANES))
                 + partner * jnp.tile(tab_ref[1], (1, tn // LANES)))
        n_rope = jnp.where(j == LAST_TILE, LANES, tn)
        o_ref[...] = jnp.where(lane < n_rope, roped, acc).astype(BF16)


def _inproj_call(h, mod3, norm_g, w, tabs, l, seq, tm):
    rows, d = h.shape
    nqb = seq // tm
    return pl.pallas_call(
        _inproj_kernel,
        out_shape=jax.ShapeDtypeStruct((rows, NP), BF16),
        grid=(rows // tm, NP // TN),
        in_specs=[
            pl.BlockSpec((tm, d), lambda i, j: (i, 0)),
            pl.BlockSpec((1, 1, d), lambda i, j: (l * MOD_ROWS + i // nqb, 0, 0)),
            pl.BlockSpec((1, 1, d), lambda i, j: (l * MOD_ROWS + i // nqb, 0, 1)),
            pl.BlockSpec((1, 1, d), lambda i, j: (l, 0, 0)),
            pl.BlockSpec((1, d, TN), lambda i, j: (l, 0, j)),
            pl.BlockSpec((2, tm, LANES), lambda i, j: (0, i % nqb, 0)),
        ],
        out_specs=pl.BlockSpec((tm, TN), lambda i, j: (i, j)),
        scratch_shapes=[pltpu.VMEM((tm, d), BF16)],
        compiler_params=_cparams(2),
        name="norm_inproj_rope",
    )(h, mod3, mod3, norm_g, w, tabs)


def _rope_tables(seq):
    half = HEAD_DIM // 2
    inv = 1.0 / (ROPE_THETA ** (jnp.arange(0, HEAD_DIM, 2, dtype=F32) / HEAD_DIM))
    ang = jnp.arange(seq, dtype=F32)[:, None] * inv[None, :]
    lane = np.arange(LANES)
    sign = np.where((lane % HEAD_DIM) < half, -1.0, 1.0).astype(np.float32)
    return jnp.stack([jnp.cos(ang)[:, lane % half], jnp.sin(ang)[:, lane % half] * sign[None, :]])


_SRC = dict(zip(_SPLIT_NAMES, np.cumsum((0,) + _SPLIT_SIZES[:-1]).tolist()))
_WHOLE = ((C_MA, "ma"), (C_MB, "mb"), (C_MC, "mc"), (C_AV, "av"), (C_AG, "ag"), (C_BG, "bg"),
          (C_CQ, "cq"), (C_CK, "ck"), (C_CV, "cv"), (C_CG, "cg"), (C_AQ, "aq"), (C_AK, "ak"),
          (C_BQ, "bq"), (C_IQ, "iq"), (C_BK2, "bk"), (C_BV2, "bv"))
_HALVES = ((C_IK2, _SRC["ik"], _SRC["ik"]),
           (C_BK2S, _SRC["bk"] + HEAD_DIM, _SRC["bk"]),
           (C_BV2S, _SRC["bv"] + HEAD_DIM, _SRC["bv"]))


def _prep_kernel(w_ref, o_ref):
    rows = w_ref.shape[1]
    lane = lax.broadcasted_iota(I32, (rows, LANES), 1)
    sizes = dict(zip(_SPLIT_NAMES, _SPLIT_SIZES))
    for dst, name in _WHOLE:
        o_ref[0, :, dst:dst + sizes[name]] = w_ref[0, :, _SRC[name]:_SRC[name] + sizes[name]].astype(BF16)
    for dst, lo, hi in _HALVES:
        low = w_ref[0, :, lo:lo + LANES]
        high = w_ref[0, :, hi - HEAD_DIM:hi - HEAD_DIM + LANES]
        o_ref[0, :, dst:dst + LANES] = jnp.where(lane < HEAD_DIM, low, high).astype(BF16)
    iw = w_ref[0, :, _SRC["iw"]:_SRC["iw"] + LANES]
    o_ref[0, :, C_IW:C_IW + LANES] = jnp.where(lane < sizes["iw"], iw, 0.0).astype(BF16)


def _prep_call(w):
    depth, d, n = w.shape
    rows = 128
    return pl.pallas_call(
        _prep_kernel,
        out_shape=jax.ShapeDtypeStruct((depth, d, NP), BF16),
        grid=(depth, d // rows),
        in_specs=[pl.BlockSpec((1, rows, n), lambda l, i: (l, i, 0))],
        out_specs=pl.BlockSpec((1, rows, NP), lambda l, i: (l, i, 0)),
        compiler_params=_cparams(2),
        name="inproj_weight_layout",
    )(w)


def _gated_store(o_ref, g_ref, p, val):
    sl = slice(p * LANES, (p + 1) * LANES)
    g = g_ref[:, sl].astype(F32)
    o_ref[:, sl] = (val * (g * _sigmoid(g))).astype(o_ref.dtype)


def _dsa_kernel(q_ref, k_ref, v_ref, g_ref, iq_ref, ik_ref, iw_ref, o_ref,
                keys_sc, qm_sc, iqm_sc, w_sc, m_sc, l_sc, acc_sc, *, tq, ck, topk):
    i = pl.program_id(1)
    nck = (i + 1) * (tq // ck)
    row0 = i * tq
    rep = ck // LANES
    lo_half = _lo_half(tq)
    kf = jnp.float32(topk)

    for p in range(N_PAIRS):
        qp = q_ref[:, p * LANES:(p + 1) * LANES].astype(F32) * (HEAD_DIM ** -0.5)
        qm_sc[2 * p] = jnp.where(lo_half, qp, 0.0).astype(BF16)
        qm_sc[2 * p + 1] = jnp.where(lo_half, 0.0, qp).astype(BF16)
    for p in range(IDX_HEADS // 2):
        t = iq_ref[:, p * LANES:(p + 1) * LANES].astype(F32)
        iqm_sc[2 * p] = jnp.where(lo_half, t, 0.0).astype(BF16)
        iqm_sc[2 * p + 1] = jnp.where(lo_half, 0.0, t).astype(BF16)
    iw = iw_ref[...].astype(F32) * ((IDX_HEADS ** -0.5) * (HEAD_DIM ** -0.5))
    for h in range(IDX_HEADS):
        w_sc[h] = jnp.broadcast_to(iw[:, h:h + 1], (tq, LANES))

    row = row0 + lax.broadcasted_iota(I32, (tq, ck), 0)
    col_l = lax.broadcasted_iota(I32, (tq, ck), 1)

    def score_chunk(c, carry):
        k0 = pl.multiple_of(c * ck, ck)
        ikc = ik_ref[pl.ds(k0, ck), :]
        sc = jnp.zeros((tq, ck), F32)
        for h in range(IDX_HEADS):
            d = _dot_nt(iqm_sc[h], ikc)
            sc = sc + jnp.tile(w_sc[h], (1, rep)) * jnp.maximum(d, 0.0)
        bits = pltpu.bitcast(sc, I32)
        key = bits ^ ((bits >> 31) & 0x7FFFFFFF)
        keys_sc[c] = jnp.where(k0 + col_l <= row, key, INT_MIN)
        return carry

    lax.fori_loop(0, nck, score_chunk, 0)

    def count_ge(thr):
        thr_w = jnp.tile(thr, (1, rep))

        def body(c, part):
            ones = jnp.where(keys_sc[c] >= thr_w, 1.0, 0.0)
            f = ones[:, :LANES]
            for t in range(1, rep):
                f = f + ones[:, t * LANES:(t + 1) * LANES]
            return part + f

        part = lax.fori_loop(0, nck, body, jnp.zeros((tq, LANES), F32))
        return jnp.sum(part, axis=1, keepdims=True)

    def bisect(it, res):
        cand = res + jnp.left_shift(jnp.int32(1), 31 - it)
        return jnp.where(count_ge(cand) >= kf, cand, res)

    res = lax.fori_loop(0, 32, bisect, jnp.full((tq, LANES), INT_MIN, I32))
    tau = jnp.maximum(res, INT_MIN + 1)
    n_ge = count_ge(tau)

    @pl.when(jnp.max(n_ge) > kf)
    def _():
        need = kf - count_ge(tau + 1)
        tau_w = jnp.tile(tau, (1, rep))
        incl = (lax.broadcasted_iota(I32, (ck, ck), 0)
                <= lax.broadcasted_iota(I32, (ck, ck), 1)).astype(F32).astype(BF16)

        def body(c, seen):
            kc = keys_sc[c]
            eq = jnp.where(kc == tau_w, 1.0, 0.0)
            rank = jnp.dot(eq.astype(BF16), incl, preferred_element_type=F32) + jnp.tile(seen, (1, rep))
            drop = eq * jnp.where(rank > need, 1.0, 0.0)
            keys_sc[c] = jnp.where(drop > 0.5, tau_w - 1, kc)
            return seen + jnp.sum(eq, axis=1, keepdims=True)

        lax.fori_loop(0, nck, body, jnp.zeros((tq, LANES), F32))

    m_sc[...] = jnp.full(m_sc.shape, NEG, F32)
    l_sc[...] = jnp.zeros(l_sc.shape, F32)
    acc_sc[...] = jnp.zeros(acc_sc.shape, F32)
    tau_w = jnp.tile(tau, (1, rep))
    ones_v = jnp.ones((ck, LANES), BF16)

    def attend(c, carry):
        k0 = pl.multiple_of(c * ck, ck)
        bias = jnp.where(keys_sc[c] >= tau_w, 0.0, NEG)
        for p in range(N_PAIRS):
            sl = slice(p * LANES, (p + 1) * LANES)
            kp = k_ref[pl.ds(k0, ck), sl]
            vp = jnp.concatenate([v_ref[pl.ds(k0, ck), sl], ones_v], axis=1)
            for e in range(2):
                h = 2 * p + e
                s = _dot_nt(qm_sc[h], kp) + bias
                m_prev = m_sc[h]
                m_next = jnp.maximum(m_prev, jnp.max(s, axis=1, keepdims=True))
                pe = jnp.exp(s - jnp.tile(m_next, (1, rep)))
                alpha = jnp.exp(m_prev - m_next)
                pv = jnp.dot(pe.astype(BF16), vp, preferred_element_type=F32)
                l_sc[h] = alpha * l_sc[h] + pv[:, LANES:]
                acc_sc[h] = alpha * acc_sc[h] + pv[:, :LANES]
                m_sc[h] = m_next
        return carry

    lax.fori_loop(0, nck, attend, 0)

    for p in range(N_PAIRS):
        out = jnp.where(lo_half, acc_sc[2 * p] / l_sc[2 * p], acc_sc[2 * p + 1] / l_sc[2 * p + 1])
        _gated_store(o_ref, g_ref, p, out)


def _dsa_call(z, batch, seq, tq, ck):
    topk = min(TOPK_MAX, seq // 4)
    nq = seq // tq
    kern = functools.partial(_dsa_kernel, tq=tq, ck=ck, topk=topk)
    return pl.pallas_call(
        kern,
        out_shape=jax.ShapeDtypeStruct((batch * seq, MIX_W), BF16),
        grid=(batch, nq),
        in_specs=[
            pl.BlockSpec((tq, MIX_W), lambda b, i: (b * nq + i, C_AQ // MIX_W)),
            pl.BlockSpec((seq, MIX_W), lambda b, i: (b, C_AK // MIX_W)),
            pl.BlockSpec((seq, MIX_W), lambda b, i: (b, C_AV // MIX_W)),
            pl.BlockSpec((tq, MIX_W), lambda b, i: (b * nq + i, C_AG // MIX_W)),
            pl.BlockSpec((tq, 2 * LANES), lambda b, i: (b * nq + i, C_IQ // (2 * LANES))),
            pl.BlockSpec((seq, LANES), lambda b, i: (b, C_IK2 // LANES)),
            pl.BlockSpec((tq, LANES), lambda b, i: (b * nq + i, C_IW // LANES)),
        ],
        out_specs=pl.BlockSpec((tq, MIX_W), lambda b, i: (b * nq + i, 0)),
        scratch_shapes=[
            pltpu.VMEM((seq // ck, tq, ck), I32),
            pltpu.VMEM((N_HEADS, tq, LANES), BF16),
            pltpu.VMEM((IDX_HEADS, tq, LANES), BF16),
            pltpu.VMEM((IDX_HEADS, tq, LANES), F32),
            pltpu.VMEM((N_HEADS, tq, LANES), F32),
            pltpu.VMEM((N_HEADS, tq, LANES), F32),
            pltpu.VMEM((N_HEADS, tq, LANES), F32),
        ],
        compiler_params=_cparams(2),
        name="dsa_attention",
    )(z, z, z, z, z, z, z)


def _swa_kernel(sinks_ref, q_ref, k2_ref, k2s_ref, v2_ref, v2s_ref, g_ref, o_ref, *, tq, layer):
    i = pl.program_id(1)
    span = tq + WINDOW
    kstart = pl.multiple_of(jnp.maximum(i * tq - WINDOW, 0), WINDOW)
    lo_half = _lo_half(tq)
    row = i * tq + lax.broadcasted_iota(I32, (tq, span), 0)
    col = kstart + lax.broadcasted_iota(I32, (tq, span), 1)
    bias = jnp.where((col <= row) & (col > row - WINDOW), 0.0, NEG)
    kv = {False: (k2s_ref[pl.ds(kstart, span), :], v2s_ref[pl.ds(kstart, span), :]),
          True: (k2_ref[pl.ds(kstart, span), :], v2_ref[pl.ds(kstart, span), :])}
    grp = N_HEADS // B_KV_HEADS
    for p in range(N_PAIRS):
        qp = q_ref[:, p * LANES:(p + 1) * LANES].astype(F32) * (HEAD_DIM ** -0.5)
        halves = []
        for e in range(2):
            h = 2 * p + e
            qm = jnp.where(lo_half if e == 0 else ~lo_half, qp, 0.0).astype(BF16)
            kk, vv = kv[(h // grp) == e]
            s = _dot_nt(qm, kk) + bias
            sink = sinks_ref[layer, h]
            m = jnp.maximum(jnp.max(s, axis=1, keepdims=True), sink)
            pe = jnp.exp(s - m)
            den = jnp.sum(pe, axis=1, keepdims=True) + jnp.exp(sink - m)
            halves.append(jnp.dot(pe.astype(BF16), vv, preferred_element_type=F32) / den)
        _gated_store(o_ref, g_ref, p, jnp.where(lo_half, halves[0], halves[1]))


def _swa_call(z, sinks, layer, batch, seq, tq):
    nq = seq // tq
    kv_spec = lambda c: pl.BlockSpec((seq, LANES), lambda b, i: (b, c // LANES))
    return pl.pallas_call(
        functools.partial(_swa_kernel, tq=tq, layer=layer),
        out_shape=jax.ShapeDtypeStruct((batch * seq, MIX_W), BF16),
        grid=(batch, nq),
        in_specs=[
            pl.BlockSpec(memory_space=pltpu.SMEM),
            pl.BlockSpec((tq, MIX_W), lambda b, i: (b * nq + i, C_BQ // MIX_W)),
            kv_spec(C_BK2), kv_spec(C_BK2S), kv_spec(C_BV2), kv_spec(C_BV2S),
            pl.BlockSpec((tq, MIX_W), lambda b, i: (b * nq + i, C_BG // MIX_W)),
        ],
        out_specs=pl.BlockSpec((tq, MIX_W), lambda b, i: (b * nq + i, 0)),
        compiler_params=_cparams(2),
        name="swa_sinks_attention",
    )(sinks, z, z, z, z, z, z)


def _sb_kernel(q_ref, k_ref, v_ref, g_ref, o_ref, qm_sc, run_sc, acc_sc, *, tq, ck):
    i = pl.program_id(1)
    ndiag = tq // ck
    nck = (i + 1) * ndiag
    rep = ck // LANES
    lo_half = _lo_half(tq)
    row = i * tq + lax.broadcasted_iota(I32, (tq, ck), 0)
    col_l = lax.broadcasted_iota(I32, (tq, ck), 1)
    j_idx = lax.broadcasted_iota(I32, (2 * ck, ck), 0)
    j_idx = jnp.where(j_idx >= ck, j_idx - ck, j_idx)
    neg_later2 = jnp.where(j_idx > lax.broadcasted_iota(I32, (2 * ck, ck), 1),
                           -1.0, 0.0).astype(BF16)

    for p in range(N_PAIRS):
        qp = q_ref[:, p * LANES:(p + 1) * LANES].astype(F32) * (HEAD_DIM ** -0.5)
        qm_sc[2 * p] = jnp.where(lo_half, qp, 0.0).astype(BF16)
        qm_sc[2 * p + 1] = jnp.where(lo_half, 0.0, qp).astype(BF16)
    run_sc[...] = jnp.zeros(run_sc.shape, F32)
    acc_sc[...] = jnp.zeros(acc_sc.shape, F32)

    def chunk(t, masked):
        k0 = pl.multiple_of((nck - 1 - t) * ck, ck)
        strict = (k0 + col_l < row) if masked else None
        zs, nlms, lbs, afters = {}, {}, {}, {}

        def scores(h):
            kp = k_ref[pl.ds(k0, ck), (h // 2) * LANES:(h // 2 + 1) * LANES]
            zs[h] = _dot_nt(qm_sc[h], kp) * LOG2E

        def logs(h):
            z2 = zs.pop(h)
            nlm = jnp.maximum(z2, 0.0) + jnp.log2(1.0 + jnp.exp2(-jnp.abs(z2)))
            lbs[h] = z2 - nlm
            nlms[h] = jnp.where(strict, nlm, 0.0) if masked else nlm

        def cumsum(h):
            hi = nlms[h].astype(BF16)
            lo = (nlms[h] - hi.astype(F32)).astype(BF16)
            afters[h] = jnp.dot(jnp.concatenate([hi, lo], axis=1), neg_later2,
                                preferred_element_type=F32)

        def weights(h):
            run = run_sc[h]
            a = jnp.exp2(lbs.pop(h) + afters.pop(h) + jnp.tile(run, (1, rep)))
            if masked:
                a = jnp.where(strict, a, 0.0)
            vp = v_ref[pl.ds(k0, ck), (h // 2) * LANES:(h // 2 + 1) * LANES]
            acc_sc[h] += jnp.dot(a.astype(BF16), vp, preferred_element_type=F32)
            run_sc[h] = run - jnp.sum(nlms.pop(h), axis=1, keepdims=True)

        stages = (scores, logs, cumsum, weights)
        for step in range(N_HEADS + len(stages) - 1):
            for s, stage in enumerate(stages):
                if 0 <= step - s < N_HEADS:
                    stage(step - s)

    for t in range(ndiag):
        chunk(t, True)

    def body(t, carry):
        chunk(t, False)
        return carry

    lax.fori_loop(ndiag, nck, body, 0)

    for p in range(N_PAIRS):
        _gated_store(o_ref, g_ref, p, jnp.where(lo_half, acc_sc[2 * p], acc_sc[2 * p + 1]))


def _sb_call(z, batch, seq, tq, ck):
    nq = seq // tq
    state = pltpu.VMEM((N_HEADS, tq, LANES), F32)
    return pl.pallas_call(
        functools.partial(_sb_kernel, tq=tq, ck=ck),
        out_shape=jax.ShapeDtypeStruct((batch * seq, MIX_W), BF16),
        grid=(batch, nq),
        in_specs=[
            pl.BlockSpec((tq, MIX_W), lambda b, i: (b * nq + i, C_CQ // MIX_W)),
            pl.BlockSpec((seq, MIX_W), lambda b, i: (b, C_CK // MIX_W)),
            pl.BlockSpec((seq, MIX_W), lambda b, i: (b, C_CV // MIX_W)),
            pl.BlockSpec((tq, MIX_W), lambda b, i: (b * nq + i, C_CG // MIX_W)),
        ],
        out_specs=pl.BlockSpec((tq, MIX_W), lambda b, i: (b * nq + i, 0)),
        scratch_shapes=[pltpu.VMEM((N_HEADS, tq, LANES), BF16), state, state],
        compiler_params=_cparams(2),
        name="stick_breaking_attention",
    )(z, z, z, z)


def _merge_kernel(ya_ref, yb_ref, yc_ref, ma_ref, mb_ref, mc_ref, x_ref, gate_ref,
                  wa_ref, wb_ref, wc_ref, wo_ref, fg_ref, o_ref, *, final):
    def branch(y_ref, m_ref, w_ref):
        pr = jnp.dot(y_ref[...], w_ref[0], preferred_element_type=F32)
        return _sigmoid(m_ref[...].astype(F32)) * pr

    merged = (branch(ya_ref, ma_ref, wa_ref) + branch(yb_ref, mb_ref, wb_ref)
              + branch(yc_ref, mc_ref, wc_ref))
    out = x_ref[...] + gate_ref[0] * jnp.dot(merged.astype(BF16), wo_ref[0],
                                             preferred_element_type=F32)
    if final:
        out = out * lax.rsqrt(jnp.mean(out * out, axis=-1, keepdims=True) + EPS) * fg_ref[...]
    o_ref[...] = out


def _merge_call(ya, yb, yc, z, h, mod3, wa, wb, wc, wo, fg, l, seq, tm, final):
    rows, d = h.shape
    nqb = seq // tm
    y_spec = pl.BlockSpec((tm, MIX_W), lambda i: (i, 0))
    m_spec = lambda c: pl.BlockSpec((tm, d), lambda i: (i, c // d))
    w_spec = pl.BlockSpec((1, MIX_W, d), lambda i: (l, 0, 0))
    return pl.pallas_call(
        functools.partial(_merge_kernel, final=final),
        out_shape=jax.ShapeDtypeStruct((rows, d), F32),
        grid=(rows // tm,),
        in_specs=[
            y_spec, y_spec, y_spec, m_spec(C_MA), m_spec(C_MB), m_spec(C_MC),
            pl.BlockSpec((tm, d), lambda i: (i, 0)),
            pl.BlockSpec((1, 1, d), lambda i: (l * MOD_ROWS + i // nqb, 0, 2)),
            w_spec, w_spec, w_spec,
            pl.BlockSpec((1, d, d), lambda i: (l, 0, 0)),
            pl.BlockSpec((1, d), lambda i: (0, 0)),
        ],
        out_specs=pl.BlockSpec((tm, d), lambda i: (i, 0)),
        compiler_params=_cparams(1),
        name="merge_outproj_residual",
    )(ya, yb, yc, z, z, z, h, mod3, wa, wb, wc, wo, fg.reshape(1, d))


def _tiles(seq):
    return min(1024, seq), min(256, seq), min(512, seq)


def kernel(x, c, norm_g, w_ada, b_ada, w_in, sinks, w_br_a, w_br_b, w_br_c, w_out, final_g):
    batch, seq, d = x.shape
    depth = w_in.shape[0]
    assert d == D_MODEL and seq % 256 == 0 and seq >= 256 + WINDOW and batch <= MOD_ROWS
    tm_in, tq, tm_mg = _tiles(seq)

    c_pad = jnp.zeros((MOD_ROWS, d), F32).at[:batch].set(c)
    mod3 = _mod_call(c_pad, w_ada, b_ada).reshape(depth * MOD_ROWS, 1, 3 * d)
    tabs = _rope_tables(seq)
    w_all = _prep_call(w_in)
    norm_g3 = norm_g.reshape(depth, 1, d)
    wa, wb, wc, wo = (w.astype(BF16) for w in (w_br_a, w_br_b, w_br_c, w_out))
    h = x.reshape(batch * seq, d)
    for l in range(depth):
        z = _inproj_call(h, mod3, norm_g3, w_all, tabs, l, seq, tm_in)
        ya = _dsa_call(z, batch, seq, tq, tq)
        yb = _swa_call(z, sinks, l, batch, seq, tq)
        yc = _sb_call(z, batch, seq, tq, tq)
        h = _merge_call(ya, yb, yc, z, h, mod3, wa, wb, wc, wo, final_g, l, seq, tm_mg,
                        final=(l == depth - 1))
    return h.reshape(batch, seq, d)
```

```python
import functools

import jax
import jax.numpy as jnp
import numpy as np
from jax import lax
from jax.experimental import pallas as pl
from jax.experimental.pallas import tpu as pltpu

F32 = jnp.float32
BF16 = jnp.bfloat16
I32 = jnp.int32
I16 = jnp.int16

D_MODEL = 1024
HEAD_DIM = 64
ROPE_THETA = 10000.0
EPS = 1e-6
N_HEADS = 8
N_PAIRS = N_HEADS // 2
IDX_HEADS = 4
TOPK_MAX = 256
B_KV_HEADS = 2
WINDOW = 128
MIX_W = N_HEADS * HEAD_DIM

LANES = 128
MOD_ROWS = 8
VMEM_LIMIT = 48 * 1024 * 1024
NEG = -1e30
LOG2E = 1.4426950408889634
RUN_FLOOR = -160.0
INT_MIN = -(2 ** 31)
I16_MIN = -(2 ** 15)
SUB32, SUB16 = 8, 16

C_MA, C_MB, C_MC = 0, 1024, 2048
C_AV, C_AG, C_BG, C_CQ, C_CK, C_CV, C_CG = 3072, 3584, 4096, 4608, 5120, 5632, 6144
C_AQ, C_AK, C_BQ = 6656, 7168, 7680
C_IQ = 8192
C_IK2, C_BK2, C_BK2S, C_BV2, C_BV2S, C_IW = 8448, 8576, 8704, 8832, 8960, 9088
NP = 9216
TN = 512
FIRST_ROPE_TILE = C_AQ // TN
LAST_TILE = NP // TN - 1
assert C_BK2S == LAST_TILE * TN and C_AQ % TN == 0

_SPLIT_NAMES = ("aq", "ak", "av", "ag", "iq", "ik", "iw", "bq", "bk", "bv", "bg",
                "cq", "ck", "cv", "cg", "ma", "mb", "mc")
_SPLIT_SIZES = (512, 512, 512, 512, 256, 64, 4, 512, 128, 128, 512,
                512, 512, 512, 512, 1024, 1024, 1024)


def _cparams(n_grid):
    return pltpu.CompilerParams(dimension_semantics=("arbitrary",) * n_grid,
                                vmem_limit_bytes=VMEM_LIMIT)


def _sigmoid(x):
    return 1.0 / (1.0 + jnp.exp(-x))


def _dot_nt(a, b):
    return lax.dot_general(a, b, (((1,), (1,)), ((), ())), preferred_element_type=F32)


def _lo_half(rows):
    return lax.broadcasted_iota(I32, (rows, LANES), 1) < HEAD_DIM


def _mod_kernel(c_ref, w_ref, b_ref, o_ref):
    c = c_ref[...]
    cs = c * _sigmoid(c)
    o_ref[0] = jnp.dot(cs, w_ref[0], preferred_element_type=F32,
                       precision=lax.Precision.HIGHEST) + b_ref[0]


def _mod_call(c_pad, w_ada, b_ada):
    depth, d, d3 = w_ada.shape
    rows = c_pad.shape[0]
    return pl.pallas_call(
        _mod_kernel,
        out_shape=jax.ShapeDtypeStruct((depth, rows, d3), F32),
        grid=(depth, d3 // d),
        in_specs=[
            pl.BlockSpec((rows, d), lambda l, j: (0, 0)),
            pl.BlockSpec((1, d, d), lambda l, j: (l, 0, j)),
            pl.BlockSpec((1, 1, d), lambda l, j: (l, 0, j)),
        ],
        out_specs=pl.BlockSpec((1, rows, d), lambda l, j: (l, 0, j)),
        compiler_params=_cparams(2),
        name="adaln_mod",
    )(c_pad, w_ada, b_ada.reshape(depth, 1, d3))


def _inproj_kernel(x_ref, shift_ref, scale_ref, g_ref, w_ref, tab_ref, o_ref, u_sc):
    j = pl.program_id(1)

    @pl.when(j == 0)
    def _():
        x = x_ref[...]
        y = x * lax.rsqrt(jnp.mean(x * x, axis=-1, keepdims=True) + EPS) * g_ref[0]
        u_sc[...] = (y * (1.0 + scale_ref[0]) + shift_ref[0]).astype(BF16)

    acc = jnp.dot(u_sc[...], w_ref[0], preferred_element_type=F32)

    @pl.when(j < FIRST_ROPE_TILE)
    def _():
        o_ref[...] = acc.astype(BF16)

    @pl.when(j >= FIRST_ROPE_TILE)
    def _():
        tm, tn = acc.shape
        lane = lax.broadcasted_iota(I32, (tm, tn), 1)
        first = (lane & (HEAD_DIM - 1)) < HEAD_DIM // 2
        partner = jnp.where(first, pltpu.roll(acc, tn - HEAD_DIM // 2, 1),
                            pltpu.roll(acc, HEAD_DIM // 2, 1))
        roped = (acc * jnp.tile(tab_ref[0], (1, tn // LANES))
                 + partner * jnp.tile(tab_ref[1], (1, tn // LANES)))
        n_rope = jnp.where(j == LAST_TILE, LANES, tn)
        o_ref[...] = jnp.where(lane < n_rope, roped, acc).astype(BF16)


def _inproj_call(h, mod3, norm_g, w, tabs, l, seq, tm):
    rows, d = h.shape
    nqb = seq // tm
    return pl.pallas_call(
        _inproj_kernel,
        out_shape=jax.ShapeDtypeStruct((rows, NP), BF16),
        grid=(rows // tm, NP // TN),
        in_specs=[
            pl.BlockSpec((tm, d), lambda i, j: (i, 0)),
            pl.BlockSpec((1, 1, d), lambda i, j: (l * MOD_ROWS + i // nqb, 0, 0)),
            pl.BlockSpec((1, 1, d), lambda i, j: (l * MOD_ROWS + i // nqb, 0, 1)),
            pl.BlockSpec((1, 1, d), lambda i, j: (l, 0, 0)),
            pl.BlockSpec((1, d, TN), lambda i, j: (l, 0, j)),
            pl.BlockSpec((2, tm, LANES), lambda i, j: (0, i % nqb, 0)),
        ],
        out_specs=pl.BlockSpec((tm, TN), lambda i, j: (i, j)),
        scratch_shapes=[pltpu.VMEM((tm, d), BF16)],
        compiler_params=_cparams(2),
        name="norm_inproj_rope",
    )(h, mod3, mod3, norm_g, w, tabs)


def _rope_tables(seq):
    half = HEAD_DIM // 2
    inv = 1.0 / (ROPE_THETA ** (jnp.arange(0, HEAD_DIM, 2, dtype=F32) / HEAD_DIM))
    ang = jnp.arange(seq, dtype=F32)[:, None] * inv[None, :]
    lane = np.arange(LANES)
    sign = np.where((lane % HEAD_DIM) < half, -1.0, 1.0).astype(np.float32)
    return jnp.stack([jnp.cos(ang)[:, lane % half], jnp.sin(ang)[:, lane % half] * sign[None, :]])


_SRC = dict(zip(_SPLIT_NAMES, np.cumsum((0,) + _SPLIT_SIZES[:-1]).tolist()))
_WHOLE = ((C_MA, "ma"), (C_MB, "mb"), (C_MC, "mc"), (C_AV, "av"), (C_AG, "ag"), (C_BG, "bg"),
          (C_CQ, "cq"), (C_CK, "ck"), (C_CV, "cv"), (C_CG, "cg"), (C_AQ, "aq"), (C_AK, "ak"),
          (C_BQ, "bq"), (C_IQ, "iq"), (C_BK2, "bk"), (C_BV2, "bv"))
_HALVES = ((C_IK2, _SRC["ik"], _SRC["ik"]),
           (C_BK2S, _SRC["bk"] + HEAD_DIM, _SRC["bk"]),
           (C_BV2S, _SRC["bv"] + HEAD_DIM, _SRC["bv"]))


def _prep_kernel(w_ref, o_ref):
    rows = w_ref.shape[1]
    lane = lax.broadcasted_iota(I32, (rows, LANES), 1)
    sizes = dict(zip(_SPLIT_NAMES, _SPLIT_SIZES))
    for dst, name in _WHOLE:
        o_ref[0, :, dst:dst + sizes[name]] = w_ref[0, :, _SRC[name]:_SRC[name] + sizes[name]].astype(BF16)
    for dst, lo, hi in _HALVES:
        low = w_ref[0, :, lo:lo + LANES]
        high = w_ref[0, :, hi - HEAD_DIM:hi - HEAD_DIM + LANES]
        o_ref[0, :, dst:dst + LANES] = jnp.where(lane < HEAD_DIM, low, high).astype(BF16)
    iw = w_ref[0, :, _SRC["iw"]:_SRC["iw"] + LANES]
    o_ref[0, :, C_IW:C_IW + LANES] = jnp.where(lane < sizes["iw"], iw, 0.0).astype(BF16)


def _prep_call(w):
    depth, d, n = w.shape
    rows = 128
    return pl.pallas_call(
        _prep_kernel,
        out_shape=jax.ShapeDtypeStruct((depth, d, NP), BF16),
        grid=(depth, d // rows),
        in_specs=[pl.BlockSpec((1, rows, n), lambda l, i: (l, i, 0))],
        out_specs=pl.BlockSpec((1, rows, NP), lambda l, i: (l, i, 0)),
        compiler_params=_cparams(2),
        name="inproj_weight_layout",
    )(w)


def _gated_store(o_ref, g_ref, p, val):
    sl = slice(p * LANES, (p + 1) * LANES)
    g = g_ref[:, sl].astype(F32)
    o_ref[:, sl] = (val * (g * _sigmoid(g))).astype(o_ref.dtype)


def _dsa_kernel(q_ref, k_ref, v_ref, g_ref, iq_ref, ik_ref, iw_ref, o_ref,
                keys_sc, khi_sc, klo_sc, qm_sc, iqm_sc, w_sc, m_sc, l_sc, acc_sc, *, tq, ck, topk):
    i = pl.program_id(1)
    ndiag = tq // ck
    nck = (i + 1) * ndiag
    nfull = nck - ndiag
    lo_half = _lo_half(tq)
    kf = jnp.float32(topk)

    for p in range(N_PAIRS):
        qp = q_ref[:, p * LANES:(p + 1) * LANES].astype(F32) * (HEAD_DIM ** -0.5)
        qm_sc[2 * p] = jnp.where(lo_half, qp, 0.0).astype(BF16)
        qm_sc[2 * p + 1] = jnp.where(lo_half, 0.0, qp).astype(BF16)
    for p in range(IDX_HEADS // 2):
        t = iq_ref[:, p * LANES:(p + 1) * LANES].astype(F32)
        iqm_sc[2 * p] = jnp.where(lo_half, t, 0.0).astype(BF16)
        iqm_sc[2 * p + 1] = jnp.where(lo_half, 0.0, t).astype(BF16)
    w_sc[...] = iw_ref[...].astype(F32).T[:SUB32] * ((IDX_HEADS ** -0.5) * (HEAD_DIM ** -0.5))

    q_pos = i * tq + lax.broadcasted_iota(I32, (ck, tq), 1)
    k_off = lax.broadcasted_iota(I32, (ck, tq), 0)

    def score_chunks(cs, masked):
        k0s = [pl.multiple_of(c * ck, ck) for c in cs]
        ds = [[_dot_nt(ik_ref[pl.ds(k0, ck), :], iqm_sc[h]) for h in range(IDX_HEADS)] for k0 in k0s]
        for c, k0, d in zip(cs, k0s, ds):
            sc = w_sc[0:1, :] * jnp.maximum(d[0], 0.0)
            for h in range(1, IDX_HEADS):
                sc = sc + w_sc[h:h + 1, :] * jnp.maximum(d[h], 0.0)
            bits = pltpu.bitcast(sc, I32)
            key = bits ^ ((bits >> 31) & 0x7FFFFFFF)
            if masked:
                key = jnp.where(k0 + k_off <= q_pos, key, INT_MIN)
            keys_sc[c] = key
            khi_sc[c] = (key >> 16).astype(I16)

    def score_pair(c2, carry):
        score_chunks([2 * c2, 2 * c2 + 1], False)
        return carry

    lax.fori_loop(0, nfull // 2, score_pair, 0)

    @pl.when(nfull % 2 == 1)
    def _():
        score_chunks([nfull - 1], False)

    score_chunks([nfull + t for t in range(ndiag)], True)

    def count16(ref, thr, strict=False):
        def body(c, part):
            hit = (ref[c] > thr) if strict else (ref[c] >= thr)
            ones = jnp.where(hit, jnp.bfloat16(1), jnp.bfloat16(0))
            for r in range(ck // SUB16):
                part = part + ones[r * SUB16:(r + 1) * SUB16]
            return part

        part = lax.fori_loop(0, nck, body, jnp.zeros((SUB16, tq), BF16))
        return jnp.sum(part.astype(F32), axis=0, keepdims=True)

    def bisect16(ref, want):
        def step(it, res):
            cand = res + jnp.left_shift(jnp.int32(1), 15 - it)
            return jnp.where(count16(ref, cand.astype(I16)) >= want, cand, res)

        return lax.fori_loop(0, 16, step, jnp.full((1, tq), I16_MIN, I32))

    hi = bisect16(khi_sc, kf)
    hi16 = hi.astype(I16)
    want_lo = kf - count16(khi_sc, hi16, strict=True)

    def low_half(c, carry):
        lo = (keys_sc[c] ^ 0x8000).astype(I16)
        klo_sc[c] = jnp.where(khi_sc[c] == hi16, lo, jnp.int16(I16_MIN))
        return carry

    lax.fori_loop(0, nck, low_half, 0)
    lo = bisect16(klo_sc, want_lo)
    res = (hi << 16) | ((lo ^ 0x8000) & 0xFFFF)
    tau = jnp.maximum(res, INT_MIN + 1)

    def count_ge(thr):
        def body(c, part):
            ones = jnp.where(keys_sc[c] >= thr, 1.0, 0.0)
            return part + jnp.sum(ones.reshape(ck // SUB32, SUB32, tq), axis=0)

        part = lax.fori_loop(0, nck, body, jnp.zeros((SUB32, tq), F32))
        return jnp.sum(part, axis=0, keepdims=True)

    n_ge = count_ge(tau)

    @pl.when(jnp.max(n_ge) > kf)
    def _():
        need = kf - count_ge(tau + 1)
        incl = (lax.broadcasted_iota(I32, (ck, ck), 1)
                <= lax.broadcasted_iota(I32, (ck, ck), 0)).astype(F32).astype(BF16)

        def body(c, seen):
            kc = keys_sc[c]
            eq = jnp.where(kc == tau, 1.0, 0.0)
            rank = jnp.dot(incl, eq.astype(BF16), preferred_element_type=F32) + seen
            drop = eq * jnp.where(rank > need, 1.0, 0.0)
            keys_sc[c] = jnp.where(drop > 0.5, tau - 1, kc)
            return seen + jnp.sum(eq, axis=0, keepdims=True)

        lax.fori_loop(0, nck, body, jnp.zeros((1, tq), F32))

    m_sc[...] = jnp.full(m_sc.shape, NEG, F32)
    l_sc[...] = jnp.zeros(l_sc.shape, F32)
    acc_sc[...] = jnp.zeros(acc_sc.shape, F32)
    ones_rows = jnp.ones((SUB16, ck), BF16)

    def attend_chunks(cs):
        k0s = [pl.multiple_of(c * ck, ck) for c in cs]
        biases = [jnp.where(keys_sc[c] >= tau, 0.0, NEG) for c in cs]
        ss, ps, alphas, vts = {}, {}, {}, {}

        def scores(j, h):
            sl = slice((h // 2) * LANES, (h // 2 + 1) * LANES)
            ss[j, h] = _dot_nt(k_ref[pl.ds(k0s[j], ck), sl], qm_sc[h]) + biases[j]
            if h % 2 == 0:
                vts[j, h // 2] = v_ref[pl.ds(k0s[j], ck), sl].astype(F32).T.astype(BF16)

        def softmax(j, h):
            s = ss.pop((j, h))
            m_prev = m_sc[h:h + 1, :]
            m_next = jnp.maximum(m_prev, jnp.max(s, axis=0, keepdims=True))
            ps[j, h] = jnp.exp(s - m_next).astype(BF16)
            alphas[j, h] = jnp.exp(m_prev - m_next)
            m_sc[h:h + 1, :] = m_next

        def values(j, h):
            e = h % 2
            vt = jnp.concatenate([vts[j, h // 2][e * HEAD_DIM:(e + 1) * HEAD_DIM], ones_rows], axis=0)
            pv = jnp.dot(vt, ps.pop((j, h)), preferred_element_type=F32)
            alpha = alphas.pop((j, h))
            l_sc[h:h + 1, :] = alpha * l_sc[h:h + 1, :] + pv[HEAD_DIM:HEAD_DIM + 1, :]
            acc_sc[h] = alpha * acc_sc[h] + pv[:HEAD_DIM, :]

        work = [(j, h) for j in range(len(cs)) for h in range(N_HEADS)]
        stages = (scores, softmax, values)
        for step in range(len(work) + len(stages) - 1):
            for s, stage in enumerate(stages):
                if 0 <= step - s < len(work):
                    stage(*work[step - s])

    def attend_pair(c2, carry):
        attend_chunks([2 * c2, 2 * c2 + 1])
        return carry

    lax.fori_loop(0, nck // 2, attend_pair, 0)

    @pl.when(nck % 2 == 1)
    def _():
        attend_chunks([nck - 1])

    for p in range(N_PAIRS):
        he, ho = 2 * p, 2 * p + 1
        out_t = jnp.concatenate([acc_sc[he] / l_sc[he:he + 1, :],
                                 acc_sc[ho] / l_sc[ho:ho + 1, :]], axis=0)
        _gated_store(o_ref, g_ref, p, out_t.T)


def _dsa_call(z, batch, seq, tq, ck):
    topk = min(TOPK_MAX, seq // 4)
    nq = seq // tq
    assert seq // SUB16 <= 256
    kern = functools.partial(_dsa_kernel, tq=tq, ck=ck, topk=topk)
    return pl.pallas_call(
        kern,
        out_shape=jax.ShapeDtypeStruct((batch * seq, MIX_W), BF16),
        grid=(batch, nq),
        in_specs=[
            pl.BlockSpec((tq, MIX_W), lambda b, i: (b * nq + i, C_AQ // MIX_W)),
            pl.BlockSpec((seq, MIX_W), lambda b, i: (b, C_AK // MIX_W)),
            pl.BlockSpec((seq, MIX_W), lambda b, i: (b, C_AV // MIX_W)),
            pl.BlockSpec((tq, MIX_W), lambda b, i: (b * nq + i, C_AG // MIX_W)),
            pl.BlockSpec((tq, 2 * LANES), lambda b, i: (b * nq + i, C_IQ // (2 * LANES))),
            pl.BlockSpec((seq, LANES), lambda b, i: (b, C_IK2 // LANES)),
            pl.BlockSpec((tq, LANES), lambda b, i: (b * nq + i, C_IW // LANES)),
        ],
        out_specs=pl.BlockSpec((tq, MIX_W), lambda b, i: (b * nq + i, 0)),
        scratch_shapes=[
            pltpu.VMEM((seq // ck, ck, tq), I32),
            pltpu.VMEM((seq // ck, ck, tq), I16),
            pltpu.VMEM((seq // ck, ck, tq), I16),
            pltpu.VMEM((N_HEADS, tq, LANES), BF16),
            pltpu.VMEM((IDX_HEADS, tq, LANES), BF16),
            pltpu.VMEM((SUB32, tq), F32),
            pltpu.VMEM((N_HEADS, tq), F32),
            pltpu.VMEM((N_HEADS, tq), F32),
            pltpu.VMEM((N_HEADS, HEAD_DIM, tq), F32),
        ],
        compiler_params=_cparams(2),
        name="dsa_attention",
    )(z, z, z, z, z, z, z)


def _swa_kernel(sinks_ref, q_ref, k2_ref, k2s_ref, v2_ref, v2s_ref, g_ref, o_ref, *, tq, layer):
    i = pl.program_id(1)
    span = tq + WINDOW
    kstart = pl.multiple_of(jnp.maximum(i * tq - WINDOW, 0), WINDOW)
    lo_half = _lo_half(tq)
    row = i * tq + lax.broadcasted_iota(I32, (tq, span), 0)
    col = kstart + lax.broadcasted_iota(I32, (tq, span), 1)
    bias = jnp.where((col <= row) & (col > row - WINDOW), 0.0, NEG)
    kv = {False: (k2s_ref[pl.ds(kstart, span), :], v2s_ref[pl.ds(kstart, span), :]),
          True: (k2_ref[pl.ds(kstart, span), :], v2_ref[pl.ds(kstart, span), :])}
    grp = N_HEADS // B_KV_HEADS
    for p in range(N_PAIRS):
        qp = q_ref[:, p * LANES:(p + 1) * LANES].astype(F32) * (HEAD_DIM ** -0.5)
        halves = []
        for e in range(2):
            h = 2 * p + e
            qm = jnp.where(lo_half if e == 0 else ~lo_half, qp, 0.0).astype(BF16)
            kk, vv = kv[(h // grp) == e]
            s = _dot_nt(qm, kk) + bias
            sink = sinks_ref[layer, h]
            m = jnp.maximum(jnp.max(s, axis=1, keepdims=True), sink)
            pe = jnp.exp(s - m)
            den = jnp.sum(pe, axis=1, keepdims=True) + jnp.exp(sink - m)
            halves.append(jnp.dot(pe.astype(BF16), vv, preferred_element_type=F32) / den)
        _gated_store(o_ref, g_ref, p, jnp.where(lo_half, halves[0], halves[1]))


def _swa_call(z, sinks, layer, batch, seq, tq):
    nq = seq // tq
    kv_spec = lambda c: pl.BlockSpec((seq, LANES), lambda b, i: (b, c // LANES))
    return pl.pallas_call(
        functools.partial(_swa_kernel, tq=tq, layer=layer),
        out_shape=jax.ShapeDtypeStruct((batch * seq, MIX_W), BF16),
        grid=(batch, nq),
        in_specs=[
            pl.BlockSpec(memory_space=pltpu.SMEM),
            pl.BlockSpec((tq, MIX_W), lambda b, i: (b * nq + i, C_BQ // MIX_W)),
            kv_spec(C_BK2), kv_spec(C_BK2S), kv_spec(C_BV2), kv_spec(C_BV2S),
            pl.BlockSpec((tq, MIX_W), lambda b, i: (b * nq + i, C_BG // MIX_W)),
        ],
        out_specs=pl.BlockSpec((tq, MIX_W), lambda b, i: (b * nq + i, 0)),
        compiler_params=_cparams(2),
        name="swa_sinks_attention",
    )(sinks, z, z, z, z, z, z)


def _sb_kernel(q_ref, k_ref, v_ref, g_ref, o_ref, qm_sc, run_sc, acc_sc, *, tq, ck):
    i = pl.program_id(1)
    ndiag = tq // ck
    nck = (i + 1) * ndiag
    rep = ck // LANES
    lo_half = _lo_half(tq)
    row = i * tq + lax.broadcasted_iota(I32, (tq, ck), 0)
    col_l = lax.broadcasted_iota(I32, (tq, ck), 1)
    j_idx = lax.broadcasted_iota(I32, (2 * ck, ck), 0)
    j_idx = jnp.where(j_idx >= ck, j_idx - ck, j_idx)
    neg_later2 = jnp.where(j_idx > lax.broadcasted_iota(I32, (2 * ck, ck), 1),
                           -1.0, 0.0).astype(BF16)

    for p in range(N_PAIRS):
        qp = q_ref[:, p * LANES:(p + 1) * LANES].astype(F32) * (HEAD_DIM ** -0.5)
        qm_sc[2 * p] = jnp.where(lo_half, qp, 0.0).astype(BF16)
        qm_sc[2 * p + 1] = jnp.where(lo_half, 0.0, qp).astype(BF16)
    run_sc[...] = jnp.zeros(run_sc.shape, F32)
    acc_sc[...] = jnp.zeros(acc_sc.shape, F32)

    def chunk(t, masked):
        k0 = pl.multiple_of((nck - 1 - t) * ck, ck)
        strict = (k0 + col_l < row) if masked else None
        zs, nlms, lbs, afters = {}, {}, {}, {}

        def scores(h):
            kp = k_ref[pl.ds(k0, ck), (h // 2) * LANES:(h // 2 + 1) * LANES]
            zs[h] = _dot_nt(qm_sc[h], kp) * LOG2E

        def logs(h):
            z2 = zs.pop(h)
            nlm = jnp.maximum(z2, 0.0) + jnp.log2(1.0 + jnp.exp2(-jnp.abs(z2)))
            lbs[h] = z2 - nlm
            nlms[h] = jnp.where(strict, nlm, 0.0) if masked else nlm

        def cumsum(h):
            hi = nlms[h].astype(BF16)
            lo = (nlms[h] - hi.astype(F32)).astype(BF16)
            afters[h] = jnp.dot(jnp.concatenate([hi, lo], axis=1), neg_later2,
                                preferred_element_type=F32)

        def weights(h):
            run = run_sc[h]
            a = jnp.exp2(lbs.pop(h) + afters.pop(h) + jnp.tile(run, (1, rep)))
            if masked:
                a = jnp.where(strict, a, 0.0)
            vp = v_ref[pl.ds(k0, ck), (h // 2) * LANES:(h // 2 + 1) * LANES]
            acc_sc[h] += jnp.dot(a.astype(BF16), vp, preferred_element_type=F32)
            run_sc[h] = run - jnp.sum(nlms.pop(h), axis=1, keepdims=True)

        stages = (scores, logs, cumsum, weights)
        for step in range(N_HEADS + len(stages) - 1):
            for s, stage in enumerate(stages):
                if 0 <= step - s < N_HEADS:
                    stage(step - s)

    for t in range(ndiag):
        chunk(t, True)

    def live():
        return jnp.max(run_sc[...]) > RUN_FLOOR

    def body(carry):
        t, _ = carry
        chunk(t, False)
        return t + 1, live()

    lax.while_loop(lambda carry: jnp.logical_and(carry[0] < nck, carry[1]), body,
                   (jnp.int32(ndiag), live()))

    for p in range(N_PAIRS):
        _gated_store(o_ref, g_ref, p, jnp.where(lo_half, acc_sc[2 * p], acc_sc[2 * p + 1]))


def _sb_call(z, batch, seq, tq, ck):
    nq = seq // tq
    state = pltpu.VMEM((N_HEADS, tq, LANES), F32)
    return pl.pallas_call(
        functools.partial(_sb_kernel, tq=tq, ck=ck),
        out_shape=jax.ShapeDtypeStruct((batch * seq, MIX_W), BF16),
        grid=(batch, nq),
        in_specs=[
            pl.BlockSpec((tq, MIX_W), lambda b, i: (b * nq + i, C_CQ // MIX_W)),
            pl.BlockSpec((seq, MIX_W), lambda b, i: (b, C_CK // MIX_W)),
            pl.BlockSpec((seq, MIX_W), lambda b, i: (b, C_CV // MIX_W)),
            pl.BlockSpec((tq, MIX_W), lambda b, i: (b * nq + i, C_CG // MIX_W)),
        ],
        out_specs=pl.BlockSpec((tq, MIX_W), lambda b, i: (b * nq + i, 0)),
        scratch_shapes=[pltpu.VMEM((N_HEADS, tq, LANES), BF16), state, state],
        compiler_params=_cparams(2),
        name="stick_breaking_attention",
    )(z, z, z, z)


def _merge_kernel(ya_ref, yb_ref, yc_ref, ma_ref, mb_ref, mc_ref, x_ref, gate_ref,
                  wa_ref, wb_ref, wc_ref, wo_ref, fg_ref, o_ref, *, final):
    def branch(y_ref, m_ref, w_ref):
        pr = jnp.dot(y_ref[...], w_ref[0], preferred_element_type=F32)
        return _sigmoid(m_ref[...].astype(F32)) * pr

    merged = (branch(ya_ref, ma_ref, wa_ref) + branch(yb_ref, mb_ref, wb_ref)
              + branch(yc_ref, mc_ref, wc_ref))
    out = x_ref[...] + gate_ref[0] * jnp.dot(merged.astype(BF16), wo_ref[0],
                                             preferred_element_type=F32)
    if final:
        out = out * lax.rsqrt(jnp.mean(out * out, axis=-1, keepdims=True) + EPS) * fg_ref[...]
    o_ref[...] = out


def _merge_call(ya, yb, yc, z, h, mod3, wa, wb, wc, wo, fg, l, seq, tm, final):
    rows, d = h.shape
    nqb = seq // tm
    y_spec = pl.BlockSpec((tm, MIX_W), lambda i: (i, 0))
    m_spec = lambda c: pl.BlockSpec((tm, d), lambda i: (i, c // d))
    w_spec = pl.BlockSpec((1, MIX_W, d), lambda i: (l, 0, 0))
    return pl.pallas_call(
        functools.partial(_merge_kernel, final=final),
        out_shape=jax.ShapeDtypeStruct((rows, d), F32),
        grid=(rows // tm,),
        in_specs=[
            y_spec, y_spec, y_spec, m_spec(C_MA), m_spec(C_MB), m_spec(C_MC),
            pl.BlockSpec((tm, d), lambda i: (i, 0)),
            pl.BlockSpec((1, 1, d), lambda i: (l * MOD_ROWS + i // nqb, 0, 2)),
            w_spec, w_spec, w_spec,
            pl.BlockSpec((1, d, d), lambda i: (l, 0, 0)),
            pl.BlockSpec((1, d), lambda i: (0, 0)),
        ],
        out_specs=pl.BlockSpec((tm, d), lambda i: (i, 0)),
        compiler_params=_cparams(1),
        name="merge_outproj_residual",
    )(ya, yb, yc, z, z, z, h, mod3, wa, wb, wc, wo, fg.reshape(1, d))


def _tiles(seq):
    return min(1024, seq), min(256, seq), min(512, seq)


def kernel(x, c, norm_g, w_ada, b_ada, w_in, sinks, w_br_a, w_br_b, w_br_c, w_out, final_g):
    batch, seq, d = x.shape
    depth = w_in.shape[0]
    assert d == D_MODEL and seq % 256 == 0 and seq >= 256 + WINDOW and batch <= MOD_ROWS
    tm_in, tq, tm_mg = _tiles(seq)

    c_pad = jnp.zeros((MOD_ROWS, d), F32).at[:batch].set(c)
    mod3 = _mod_call(c_pad, w_ada, b_ada).reshape(depth * MOD_ROWS, 1, 3 * d)
    tabs = _rope_tables(seq)
    w_all = _prep_call(w_in)
    norm_g3 = norm_g.reshape(depth, 1, d)
    wa, wb, wc, wo = (w.astype(BF16) for w in (w_br_a, w_br_b, w_br_c, w_out))
    h = x.reshape(batch * seq, d)
    for l in range(depth):
        z = _inproj_call(h, mod3, norm_g3, w_all, tabs, l, seq, tm_in)
        ya = _dsa_call(z, batch, seq, tq, tq)
        yb = _swa_call(z, sinks, l, batch, seq, tq)
        yc = _sb_call(z, batch, seq, tq, tq)
        h = _merge_call(ya, yb, yc, z, h, mod3, wa, wb, wc, wo, final_g, l, seq, tm_mg,
                        final=(l == depth - 1))
    return h.reshape(batch, seq, d)
```

```python
import functools

import jax
import jax.numpy as jnp
import numpy as np
from jax import lax
from jax.experimental import pallas as pl
from jax.experimental.pallas import tpu as pltpu

F32 = jnp.float32
BF16 = jnp.bfloat16
I32 = jnp.int32
I16 = jnp.int16

D_MODEL = 1024
HEAD_DIM = 64
ROPE_THETA = 10000.0
EPS = 1e-6
N_HEADS = 8
N_PAIRS = N_HEADS // 2
IDX_HEADS = 4
TOPK_MAX = 256
B_KV_HEADS = 2
WINDOW = 128
MIX_W = N_HEADS * HEAD_DIM

LANES = 128
MOD_ROWS = 8
VMEM_LIMIT = 48 * 1024 * 1024
NEG = -1e30
LOG2E = 1.4426950408889634
RUN_FLOOR = -160.0
INT_MIN = -(2 ** 31)
I16_MIN = -(2 ** 15)
SUB32, SUB16 = 8, 16

C_MA, C_MB, C_MC = 0, 1024, 2048
C_AV, C_AG, C_BG, C_CQ, C_CK, C_CV, C_CG = 3072, 3584, 4096, 4608, 5120, 5632, 6144
C_AQ, C_AK, C_BQ = 6656, 7168, 7680
C_IQ = 8192
C_IK2, C_BK2, C_BK2S, C_BV2, C_BV2S, C_IW = 8448, 8576, 8704, 8832, 8960, 9088
NP = 9216
TN = 512
INPROJ_SPLIT = 4
NP_PLAIN = C_AQ
NP_ROPE = NP - NP_PLAIN
assert NP_PLAIN % TN == 0 and NP_ROPE % TN == 0 and C_BK2S == NP - TN

_SPLIT_NAMES = ("aq", "ak", "av", "ag", "iq", "ik", "iw", "bq", "bk", "bv", "bg",
                "cq", "ck", "cv", "cg", "ma", "mb", "mc")
_SPLIT_SIZES = (512, 512, 512, 512, 256, 64, 4, 512, 128, 128, 512,
                512, 512, 512, 512, 1024, 1024, 1024)


def _cparams(n_grid):
    return pltpu.CompilerParams(dimension_semantics=("arbitrary",) * n_grid,
                                vmem_limit_bytes=VMEM_LIMIT)


def _sigmoid(x):
    return 1.0 / (1.0 + jnp.exp(-x))


def _dot_nt(a, b):
    return lax.dot_general(a, b, (((1,), (1,)), ((), ())), preferred_element_type=F32)


def _lo_half(rows):
    return lax.broadcasted_iota(I32, (rows, LANES), 1) < HEAD_DIM


def _mod_kernel(c_ref, w_ref, b_ref, o_ref):
    c = c_ref[...]
    cs = c * _sigmoid(c)
    o_ref[0] = jnp.dot(cs, w_ref[0], preferred_element_type=F32,
                       precision=lax.Precision.HIGHEST) + b_ref[0]


def _mod_call(c_pad, w_ada, b_ada):
    depth, d, d3 = w_ada.shape
    rows = c_pad.shape[0]
    return pl.pallas_call(
        _mod_kernel,
        out_shape=jax.ShapeDtypeStruct((depth, rows, d3), F32),
        grid=(depth, d3 // d),
        in_specs=[
            pl.BlockSpec((rows, d), lambda l, j: (0, 0)),
            pl.BlockSpec((1, d, d), lambda l, j: (l, 0, j)),
            pl.BlockSpec((1, 1, d), lambda l, j: (l, 0, j)),
        ],
        out_specs=pl.BlockSpec((1, rows, d), lambda l, j: (l, 0, j)),
        compiler_params=_cparams(2),
        name="adaln_mod",
    )(c_pad, w_ada, b_ada.reshape(depth, 1, d3))


def _inproj_kernel(x_ref, shift_ref, scale_ref, g_ref, w_ref, *rest, rope):
    tab_ref = rest[0] if rope else None
    o_ref, u_sc = rest[-2:]
    j = pl.program_id(1)

    @pl.when(j == 0)
    def _():
        x = x_ref[...]
        y = x * lax.rsqrt(jnp.mean(x * x, axis=-1, keepdims=True) + EPS) * g_ref[0]
        u_sc[...] = (y * (1.0 + scale_ref[0]) + shift_ref[0]).astype(BF16)

    tm, tn = o_ref.shape
    sub_rows = tm // INPROJ_SPLIT
    if rope:
        lane = lax.broadcasted_iota(I32, (sub_rows, tn), 1)
        first = (lane & (HEAD_DIM - 1)) < HEAD_DIM // 2
        n_rope = jnp.where(j == pl.num_programs(1) - 1, LANES, tn)
    for r in range(INPROJ_SPLIT):
        rows = slice(r * sub_rows, (r + 1) * sub_rows)
        acc = jnp.dot(u_sc[rows, :], w_ref[0], preferred_element_type=F32)
        if rope:
            partner = jnp.where(first, pltpu.roll(acc, tn - HEAD_DIM // 2, 1),
                                pltpu.roll(acc, HEAD_DIM // 2, 1))
            roped = (acc * jnp.tile(tab_ref[0, rows, :], (1, tn // LANES))
                     + partner * jnp.tile(tab_ref[1, rows, :], (1, tn // LANES)))
            acc = jnp.where(lane < n_rope, roped, acc)
        o_ref[rows, :] = acc.astype(BF16)


def _inproj_call(h, mod3, norm_g, w, tabs, l, seq, tm, rope):
    rows, d = h.shape
    nqb = seq // tm
    col0, ncol = (NP_PLAIN // TN, NP_ROPE) if rope else (0, NP_PLAIN)
    in_specs = [
        pl.BlockSpec((tm, d), lambda i, j: (i, 0)),
        pl.BlockSpec((1, 1, d), lambda i, j: (l * MOD_ROWS + i // nqb, 0, 0)),
        pl.BlockSpec((1, 1, d), lambda i, j: (l * MOD_ROWS + i // nqb, 0, 1)),
        pl.BlockSpec((1, 1, d), lambda i, j: (l, 0, 0)),
        pl.BlockSpec((1, d, TN), lambda i, j: (l, 0, col0 + j)),
    ]
    args = [h, mod3, mod3, norm_g, w]
    if rope:
        in_specs.append(pl.BlockSpec((2, tm, LANES), lambda i, j: (0, i % nqb, 0)))
        args.append(tabs)
    return pl.pallas_call(
        functools.partial(_inproj_kernel, rope=rope),
        out_shape=jax.ShapeDtypeStruct((rows, ncol), BF16),
        grid=(rows // tm, ncol // TN),
        in_specs=in_specs,
        out_specs=pl.BlockSpec((tm, TN), lambda i, j: (i, j)),
        scratch_shapes=[pltpu.VMEM((tm, d), BF16)],
        compiler_params=_cparams(2),
        name="norm_inproj_rope" if rope else "norm_inproj",
    )(*args)


def _rope_tables(seq):
    half = HEAD_DIM // 2
    inv = 1.0 / (ROPE_THETA ** (jnp.arange(0, HEAD_DIM, 2, dtype=F32) / HEAD_DIM))
    ang = jnp.arange(seq, dtype=F32)[:, None] * inv[None, :]
    lane = np.arange(LANES)
    sign = np.where((lane % HEAD_DIM) < half, -1.0, 1.0).astype(np.float32)
    return jnp.stack([jnp.cos(ang)[:, lane % half], jnp.sin(ang)[:, lane % half] * sign[None, :]])


_SRC = dict(zip(_SPLIT_NAMES, np.cumsum((0,) + _SPLIT_SIZES[:-1]).tolist()))
_WHOLE = ((C_MA, "ma"), (C_MB, "mb"), (C_MC, "mc"), (C_AV, "av"), (C_AG, "ag"), (C_BG, "bg"),
          (C_CQ, "cq"), (C_CK, "ck"), (C_CV, "cv"), (C_CG, "cg"), (C_AQ, "aq"), (C_AK, "ak"),
          (C_BQ, "bq"), (C_IQ, "iq"), (C_BK2, "bk"), (C_BV2, "bv"))
_HALVES = ((C_IK2, _SRC["ik"], _SRC["ik"]),
           (C_BK2S, _SRC["bk"] + HEAD_DIM, _SRC["bk"]),
           (C_BV2S, _SRC["bv"] + HEAD_DIM, _SRC["bv"]))


def _prep_kernel(w_ref, o_ref):
    rows = w_ref.shape[1]
    lane = lax.broadcasted_iota(I32, (rows, LANES), 1)
    sizes = dict(zip(_SPLIT_NAMES, _SPLIT_SIZES))
    for dst, name in _WHOLE:
        o_ref[0, :, dst:dst + sizes[name]] = w_ref[0, :, _SRC[name]:_SRC[name] + sizes[name]].astype(BF16)
    for dst, lo, hi in _HALVES:
        low = w_ref[0, :, lo:lo + LANES]
        high = w_ref[0, :, hi - HEAD_DIM:hi - HEAD_DIM + LANES]
        o_ref[0, :, dst:dst + LANES] = jnp.where(lane < HEAD_DIM, low, high).astype(BF16)
    iw = w_ref[0, :, _SRC["iw"]:_SRC["iw"] + LANES]
    o_ref[0, :, C_IW:C_IW + LANES] = jnp.where(lane < sizes["iw"], iw, 0.0).astype(BF16)


def _prep_call(w):
    depth, d, n = w.shape
    rows = 128
    return pl.pallas_call(
        _prep_kernel,
        out_shape=jax.ShapeDtypeStruct((depth, d, NP), BF16),
        grid=(depth, d // rows),
        in_specs=[pl.BlockSpec((1, rows, n), lambda l, i: (l, i, 0))],
        out_specs=pl.BlockSpec((1, rows, NP), lambda l, i: (l, i, 0)),
        compiler_params=_cparams(2),
        name="inproj_weight_layout",
    )(w)


def _gated_store(o_ref, g_ref, p, val):
    sl = slice(p * LANES, (p + 1) * LANES)
    g = g_ref[:, sl].astype(F32)
    o_ref[:, sl] = (val * (g * _sigmoid(g))).astype(o_ref.dtype)


def _dsa_kernel(q_ref, k_ref, v_ref, g_ref, iq_ref, ik_ref, iw_ref, o_ref,
                keys_sc, khi_sc, klo_sc, qm_sc, iqm_sc, w_sc, m_sc, l_sc, acc_sc, *, tq, ck, topk):
    i = pl.program_id(1)
    ndiag = tq // ck
    nck = (i + 1) * ndiag
    nfull = nck - ndiag
    lo_half = _lo_half(tq)
    kf = jnp.float32(topk)

    for p in range(N_PAIRS):
        qp = q_ref[:, p * LANES:(p + 1) * LANES].astype(F32) * (HEAD_DIM ** -0.5)
        qm_sc[2 * p] = jnp.where(lo_half, qp, 0.0).astype(BF16)
        qm_sc[2 * p + 1] = jnp.where(lo_half, 0.0, qp).astype(BF16)
    for p in range(IDX_HEADS // 2):
        t = iq_ref[:, p * LANES:(p + 1) * LANES].astype(F32)
        iqm_sc[2 * p] = jnp.where(lo_half, t, 0.0).astype(BF16)
        iqm_sc[2 * p + 1] = jnp.where(lo_half, 0.0, t).astype(BF16)
    w_sc[...] = iw_ref[...].astype(F32).T[:SUB32] * ((IDX_HEADS ** -0.5) * (HEAD_DIM ** -0.5))

    q_pos = i * tq + lax.broadcasted_iota(I32, (ck, tq), 1)
    k_off = lax.broadcasted_iota(I32, (ck, tq), 0)

    def score_chunks(cs, masked):
        k0s = [pl.multiple_of(c * ck, ck) for c in cs]
        ds = [[_dot_nt(ik_ref[pl.ds(k0, ck), :], iqm_sc[h]) for h in range(IDX_HEADS)] for k0 in k0s]
        for c, k0, d in zip(cs, k0s, ds):
            sc = w_sc[0:1, :] * jnp.maximum(d[0], 0.0)
            for h in range(1, IDX_HEADS):
                sc = sc + w_sc[h:h + 1, :] * jnp.maximum(d[h], 0.0)
            bits = pltpu.bitcast(sc, I32)
            key = bits ^ ((bits >> 31) & 0x7FFFFFFF)
            if masked:
                key = jnp.where(k0 + k_off <= q_pos, key, INT_MIN)
            keys_sc[c] = key
            khi_sc[c] = (key >> 16).astype(I16)

    def score_pair(c2, carry):
        score_chunks([2 * c2, 2 * c2 + 1], False)
        return carry

    lax.fori_loop(0, nfull // 2, score_pair, 0)

    @pl.when(nfull % 2 == 1)
    def _():
        score_chunks([nfull - 1], False)

    score_chunks([nfull + t for t in range(ndiag)], True)

    npair = (nck + 1) // 2

    @pl.when(nck % 2 == 1)
    def _():
        khi_sc[nck] = jnp.full((ck, tq), I16_MIN, I16)
        klo_sc[nck] = jnp.full((ck, tq), I16_MIN, I16)

    def count16(ref, thr, strict=False):
        def body(c2, parts):
            new = []
            for j, part in enumerate(parts):
                x = ref[2 * c2 + j]
                ones = jnp.where((x > thr) if strict else (x >= thr), jnp.bfloat16(1), jnp.bfloat16(0))
                for r in range(ck // SUB16):
                    part = part + ones[r * SUB16:(r + 1) * SUB16]
                new.append(part)
            return tuple(new)

        zero = jnp.zeros((SUB16, tq), BF16)
        parts = lax.fori_loop(0, npair, body, (zero, zero))
        return jnp.sum(parts[0].astype(F32) + parts[1].astype(F32), axis=0, keepdims=True)

    def bisect16(ref, want):
        def step(it, res):
            cand = res + jnp.left_shift(jnp.int32(1), 15 - it)
            return jnp.where(count16(ref, cand.astype(I16)) >= want, cand, res)

        return lax.fori_loop(0, 16, step, jnp.full((1, tq), I16_MIN, I32))

    hi = bisect16(khi_sc, kf)
    hi16 = hi.astype(I16)
    want_lo = kf - count16(khi_sc, hi16, strict=True)

    def low_half(c, carry):
        lo = (keys_sc[c] ^ 0x8000).astype(I16)
        klo_sc[c] = jnp.where(khi_sc[c] == hi16, lo, jnp.int16(I16_MIN))
        return carry

    lax.fori_loop(0, nck, low_half, 0)
    lo = bisect16(klo_sc, want_lo)
    res = (hi << 16) | ((lo ^ 0x8000) & 0xFFFF)
    tau = jnp.maximum(res, INT_MIN + 1)

    def count_ge(thr):
        def body(c, part):
            ones = jnp.where(keys_sc[c] >= thr, 1.0, 0.0)
            return part + jnp.sum(ones.reshape(ck // SUB32, SUB32, tq), axis=0)

        part = lax.fori_loop(0, nck, body, jnp.zeros((SUB32, tq), F32))
        return jnp.sum(part, axis=0, keepdims=True)

    n_ge = count_ge(tau)

    @pl.when(jnp.max(n_ge) > kf)
    def _():
        need = kf - count_ge(tau + 1)
        incl = (lax.broadcasted_iota(I32, (ck, ck), 1)
                <= lax.broadcasted_iota(I32, (ck, ck), 0)).astype(F32).astype(BF16)

        def body(c, seen):
            kc = keys_sc[c]
            eq = jnp.where(kc == tau, 1.0, 0.0)
            rank = jnp.dot(incl, eq.astype(BF16), preferred_element_type=F32) + seen
            drop = eq * jnp.where(rank > need, 1.0, 0.0)
            keys_sc[c] = jnp.where(drop > 0.5, tau - 1, kc)
            return seen + jnp.sum(eq, axis=0, keepdims=True)

        lax.fori_loop(0, nck, body, jnp.zeros((1, tq), F32))

    m_sc[...] = jnp.full(m_sc.shape, NEG, F32)
    l_sc[...] = jnp.zeros(l_sc.shape, F32)
    acc_sc[...] = jnp.zeros(acc_sc.shape, F32)
    ones_rows = jnp.ones((SUB16, ck), BF16)

    def attend_chunks(cs):
        k0s = [pl.multiple_of(c * ck, ck) for c in cs]
        biases = [jnp.where(keys_sc[c] >= tau, 0.0, NEG) for c in cs]
        ss, ps, alphas, vts = {}, {}, {}, {}

        def scores(j, h):
            sl = slice((h // 2) * LANES, (h // 2 + 1) * LANES)
            ss[j, h] = _dot_nt(k_ref[pl.ds(k0s[j], ck), sl], qm_sc[h]) + biases[j]
            if h % 2 == 0:
                vts[j, h // 2] = v_ref[pl.ds(k0s[j], ck), sl].astype(F32).T.astype(BF16)

        def softmax(j, h):
            s = ss.pop((j, h))
            m_prev = m_sc[h:h + 1, :]
            m_next = jnp.maximum(m_prev, jnp.max(s, axis=0, keepdims=True))
            ps[j, h] = jnp.exp(s - m_next).astype(BF16)
            alphas[j, h] = jnp.exp(m_prev - m_next)
            m_sc[h:h + 1, :] = m_next

        def values(j, h):
            e = h % 2
            vt = jnp.concatenate([vts[j, h // 2][e * HEAD_DIM:(e + 1) * HEAD_DIM], ones_rows], axis=0)
            pv = jnp.dot(vt, ps.pop((j, h)), preferred_element_type=F32)
            alpha = alphas.pop((j, h))
            l_sc[h:h + 1, :] = alpha * l_sc[h:h + 1, :] + pv[HEAD_DIM:HEAD_DIM + 1, :]
            acc_sc[h] = alpha * acc_sc[h] + pv[:HEAD_DIM, :]

        work = [(j, h) for j in range(len(cs)) for h in range(N_HEADS)]
        stages = (scores, softmax, values)
        for step in range(len(work) + len(stages) - 1):
            for s, stage in enumerate(stages):
                if 0 <= step - s < len(work):
                    stage(*work[step - s])

    def attend_pair(c2, carry):
        attend_chunks([2 * c2, 2 * c2 + 1])
        return carry

    lax.fori_loop(0, nck // 2, attend_pair, 0)

    @pl.when(nck % 2 == 1)
    def _():
        attend_chunks([nck - 1])

    for p in range(N_PAIRS):
        he, ho = 2 * p, 2 * p + 1
        out_t = jnp.concatenate([acc_sc[he] / l_sc[he:he + 1, :],
                                 acc_sc[ho] / l_sc[ho:ho + 1, :]], axis=0)
        _gated_store(o_ref, g_ref, p, out_t.T)


def _dsa_call(zp, zr, batch, seq, tq, ck):
    topk = min(TOPK_MAX, seq // 4)
    nq = seq // tq
    assert seq // SUB16 <= 256
    kern = functools.partial(_dsa_kernel, tq=tq, ck=ck, topk=topk)
    return pl.pallas_call(
        kern,
        out_shape=jax.ShapeDtypeStruct((batch * seq, MIX_W), BF16),
        grid=(batch, nq),
        in_specs=[
            pl.BlockSpec((tq, MIX_W), lambda b, i: (b * nq + i, (C_AQ - NP_PLAIN) // MIX_W)),
            pl.BlockSpec((seq, MIX_W), lambda b, i: (b, (C_AK - NP_PLAIN) // MIX_W)),
            pl.BlockSpec((seq, MIX_W), lambda b, i: (b, C_AV // MIX_W)),
            pl.BlockSpec((tq, MIX_W), lambda b, i: (b * nq + i, C_AG // MIX_W)),
            pl.BlockSpec((tq, 2 * LANES), lambda b, i: (b * nq + i, (C_IQ - NP_PLAIN) // (2 * LANES))),
            pl.BlockSpec((seq, LANES), lambda b, i: (b, (C_IK2 - NP_PLAIN) // LANES)),
            pl.BlockSpec((tq, LANES), lambda b, i: (b * nq + i, (C_IW - NP_PLAIN) // LANES)),
        ],
        out_specs=pl.BlockSpec((tq, MIX_W), lambda b, i: (b * nq + i, 0)),
        scratch_shapes=[
            pltpu.VMEM((seq // ck, ck, tq), I32),
            pltpu.VMEM((seq // ck, ck, tq), I16),
            pltpu.VMEM((seq // ck, ck, tq), I16),
            pltpu.VMEM((N_HEADS, tq, LANES), BF16),
            pltpu.VMEM((IDX_HEADS, tq, LANES), BF16),
            pltpu.VMEM((SUB32, tq), F32),
            pltpu.VMEM((N_HEADS, tq), F32),
            pltpu.VMEM((N_HEADS, tq), F32),
            pltpu.VMEM((N_HEADS, HEAD_DIM, tq), F32),
        ],
        compiler_params=_cparams(2),
        name="dsa_attention",
    )(zr, zr, zp, zp, zr, zr, zr)


def _swa_kernel(sinks_ref, q_ref, k2_ref, k2s_ref, v2_ref, v2s_ref, g_ref, o_ref, *, tq, layer):
    i = pl.program_id(1)
    span = tq + WINDOW
    kstart = pl.multiple_of(jnp.maximum(i * tq - WINDOW, 0), WINDOW)
    lo_half = _lo_half(tq)
    row = i * tq + lax.broadcasted_iota(I32, (tq, span), 0)
    col = kstart + lax.broadcasted_iota(I32, (tq, span), 1)
    bias = jnp.where((col <= row) & (col > row - WINDOW), 0.0, NEG)
    kv = {False: (k2s_ref[pl.ds(kstart, span), :], v2s_ref[pl.ds(kstart, span), :]),
          True: (k2_ref[pl.ds(kstart, span), :], v2_ref[pl.ds(kstart, span), :])}
    grp = N_HEADS // B_KV_HEADS
    ss, ps, dens, outs = {}, {}, {}, {}

    def scores(h):
        p, e = divmod(h, 2)
        qp = q_ref[:, p * LANES:(p + 1) * LANES].astype(F32) * (HEAD_DIM ** -0.5)
        qm = jnp.where(lo_half if e == 0 else ~lo_half, qp, 0.0).astype(BF16)
        ss[h] = _dot_nt(qm, kv[(h // grp) == e][0]) + bias

    def softmax(h):
        s = ss.pop(h)
        sink = sinks_ref[layer, h]
        m = jnp.maximum(jnp.max(s, axis=1, keepdims=True), sink)
        pe = jnp.exp(s - m)
        dens[h] = jnp.sum(pe, axis=1, keepdims=True) + jnp.exp(sink - m)
        ps[h] = pe.astype(BF16)

    def values(h):
        vv = kv[(h // grp) == (h % 2)][1]
        outs[h] = jnp.dot(ps.pop(h), vv, preferred_element_type=F32) / dens.pop(h)
        if h % 2 == 1:
            _gated_store(o_ref, g_ref, h // 2, jnp.where(lo_half, outs.pop(h - 1), outs.pop(h)))

    stages = (scores, softmax, values)
    for step in range(N_HEADS + len(stages) - 1):
        for s_i, stage in enumerate(stages):
            if 0 <= step - s_i < N_HEADS:
                stage(step - s_i)


def _swa_call(zp, zr, sinks, layer, batch, seq, tq):
    nq = seq // tq
    kv_spec = lambda c: pl.BlockSpec((seq, LANES), lambda b, i: (b, (c - NP_PLAIN) // LANES))
    return pl.pallas_call(
        functools.partial(_swa_kernel, tq=tq, layer=layer),
        out_shape=jax.ShapeDtypeStruct((batch * seq, MIX_W), BF16),
        grid=(batch, nq),
        in_specs=[
            pl.BlockSpec(memory_space=pltpu.SMEM),
            pl.BlockSpec((tq, MIX_W), lambda b, i: (b * nq + i, (C_BQ - NP_PLAIN) // MIX_W)),
            kv_spec(C_BK2), kv_spec(C_BK2S), kv_spec(C_BV2), kv_spec(C_BV2S),
            pl.BlockSpec((tq, MIX_W), lambda b, i: (b * nq + i, C_BG // MIX_W)),
        ],
        out_specs=pl.BlockSpec((tq, MIX_W), lambda b, i: (b * nq + i, 0)),
        compiler_params=_cparams(2),
        name="swa_sinks_attention",
    )(sinks, zr, zr, zr, zr, zr, zp)


def _sb_kernel(q_ref, k_ref, v_ref, g_ref, o_ref, qm_sc, run_sc, acc_sc, *, tq, ck):
    i = pl.program_id(1)
    ndiag = tq // ck
    nck = (i + 1) * ndiag
    rep = ck // LANES
    lo_half = _lo_half(tq)
    row = i * tq + lax.broadcasted_iota(I32, (tq, ck), 0)
    col_l = lax.broadcasted_iota(I32, (tq, ck), 1)
    j_idx = lax.broadcasted_iota(I32, (2 * ck, ck), 0)
    j_idx = jnp.where(j_idx >= ck, j_idx - ck, j_idx)
    neg_later2 = jnp.where(j_idx > lax.broadcasted_iota(I32, (2 * ck, ck), 1),
                           -1.0, 0.0).astype(BF16)

    for p in range(N_PAIRS):
        qp = q_ref[:, p * LANES:(p + 1) * LANES].astype(F32) * (HEAD_DIM ** -0.5)
        qm_sc[2 * p] = jnp.where(lo_half, qp, 0.0).astype(BF16)
        qm_sc[2 * p + 1] = jnp.where(lo_half, 0.0, qp).astype(BF16)
    run_sc[...] = jnp.zeros(run_sc.shape, F32)
    acc_sc[...] = jnp.zeros(acc_sc.shape, F32)

    def chunk(t, masked):
        k0 = pl.multiple_of((nck - 1 - t) * ck, ck)
        strict = (k0 + col_l < row) if masked else None
        zs, nlms, lbs, afters = {}, {}, {}, {}

        def scores(h):
            kp = k_ref[pl.ds(k0, ck), (h // 2) * LANES:(h // 2 + 1) * LANES]
            zs[h] = _dot_nt(qm_sc[h], kp) * LOG2E

        def logs(h):
            z2 = zs.pop(h)
            nlm = jnp.maximum(z2, 0.0) + jnp.log2(1.0 + jnp.exp2(-jnp.abs(z2)))
            lbs[h] = z2 - nlm
            nlms[h] = jnp.where(strict, nlm, 0.0) if masked else nlm

        def cumsum(h):
            hi = nlms[h].astype(BF16)
            lo = (nlms[h] - hi.astype(F32)).astype(BF16)
            afters[h] = jnp.dot(jnp.concatenate([hi, lo], axis=1), neg_later2,
                                preferred_element_type=F32)

        def weights(h):
            run = run_sc[h]
            a = jnp.exp2(lbs.pop(h) + afters.pop(h) + jnp.tile(run, (1, rep)))
            if masked:
                a = jnp.where(strict, a, 0.0)
            vp = v_ref[pl.ds(k0, ck), (h // 2) * LANES:(h // 2 + 1) * LANES]
            acc_sc[h] += jnp.dot(a.astype(BF16), vp, preferred_element_type=F32)
            run_sc[h] = run - jnp.sum(nlms.pop(h), axis=1, keepdims=True)

        stages = (scores, logs, cumsum, weights)
        for step in range(N_HEADS + len(stages) - 1):
            for s, stage in enumerate(stages):
                if 0 <= step - s < N_HEADS:
                    stage(step - s)

    for t in range(ndiag):
        chunk(t, True)

    def live():
        return jnp.max(run_sc[...]) > RUN_FLOOR

    def body(carry):
        t, _ = carry
        chunk(t, False)
        return t + 1, live()

    lax.while_loop(lambda carry: jnp.logical_and(carry[0] < nck, carry[1]), body,
                   (jnp.int32(ndiag), live()))

    for p in range(N_PAIRS):
        _gated_store(o_ref, g_ref, p, jnp.where(lo_half, acc_sc[2 * p], acc_sc[2 * p + 1]))


def _sb_call(z, batch, seq, tq, ck):
    nq = seq // tq
    state = pltpu.VMEM((N_HEADS, tq, LANES), F32)
    return pl.pallas_call(
        functools.partial(_sb_kernel, tq=tq, ck=ck),
        out_shape=jax.ShapeDtypeStruct((batch * seq, MIX_W), BF16),
        grid=(batch, nq),
        in_specs=[
            pl.BlockSpec((tq, MIX_W), lambda b, i: (b * nq + i, C_CQ // MIX_W)),
            pl.BlockSpec((seq, MIX_W), lambda b, i: (b, C_CK // MIX_W)),
            pl.BlockSpec((seq, MIX_W), lambda b, i: (b, C_CV // MIX_W)),
            pl.BlockSpec((tq, MIX_W), lambda b, i: (b * nq + i, C_CG // MIX_W)),
        ],
        out_specs=pl.BlockSpec((tq, MIX_W), lambda b, i: (b * nq + i, 0)),
        scratch_shapes=[pltpu.VMEM((N_HEADS, tq, LANES), BF16), state, state],
        compiler_params=_cparams(2),
        name="stick_breaking_attention",
    )(z, z, z, z)


def _merge_kernel(ya_ref, yb_ref, yc_ref, ma_ref, mb_ref, mc_ref, x_ref, gate_ref,
                  wa_ref, wb_ref, wc_ref, wo_ref, fg_ref, o_ref, *, final):
    def branch(y_ref, m_ref, w_ref):
        pr = jnp.dot(y_ref[...], w_ref[0], preferred_element_type=F32)
        return _sigmoid(m_ref[...].astype(F32)) * pr

    merged = (branch(ya_ref, ma_ref, wa_ref) + branch(yb_ref, mb_ref, wb_ref)
              + branch(yc_ref, mc_ref, wc_ref))
    out = x_ref[...] + gate_ref[0] * jnp.dot(merged.astype(BF16), wo_ref[0],
                                             preferred_element_type=F32)
    if final:
        out = out * lax.rsqrt(jnp.mean(out * out, axis=-1, keepdims=True) + EPS) * fg_ref[...]
    o_ref[...] = out


def _merge_call(ya, yb, yc, z, h, mod3, wa, wb, wc, wo, fg, l, seq, tm, final):
    rows, d = h.shape
    nqb = seq // tm
    y_spec = pl.BlockSpec((tm, MIX_W), lambda i: (i, 0))
    m_spec = lambda c: pl.BlockSpec((tm, d), lambda i: (i, c // d))
    w_spec = pl.BlockSpec((1, MIX_W, d), lambda i: (l, 0, 0))
    return pl.pallas_call(
        functools.partial(_merge_kernel, final=final),
        out_shape=jax.ShapeDtypeStruct((rows, d), F32),
        grid=(rows // tm,),
        in_specs=[
            y_spec, y_spec, y_spec, m_spec(C_MA), m_spec(C_MB), m_spec(C_MC),
            pl.BlockSpec((tm, d), lambda i: (i, 0)),
            pl.BlockSpec((1, 1, d), lambda i: (l * MOD_ROWS + i // nqb, 0, 2)),
            w_spec, w_spec, w_spec,
            pl.BlockSpec((1, d, d), lambda i: (l, 0, 0)),
            pl.BlockSpec((1, d), lambda i: (0, 0)),
        ],
        out_specs=pl.BlockSpec((tm, d), lambda i: (i, 0)),
        compiler_params=_cparams(1),
        name="merge_outproj_residual",
    )(ya, yb, yc, z, z, z, h, mod3, wa, wb, wc, wo, fg.reshape(1, d))


def _tiles(seq):
    return min(1024, seq), min(256, seq), min(512, seq)


def kernel(x, c, norm_g, w_ada, b_ada, w_in, sinks, w_br_a, w_br_b, w_br_c, w_out, final_g):
    batch, seq, d = x.shape
    depth = w_in.shape[0]
    assert d == D_MODEL and seq % 256 == 0 and seq >= 256 + WINDOW and batch <= MOD_ROWS
    tm_in, tq, tm_mg = _tiles(seq)

    c_pad = jnp.zeros((MOD_ROWS, d), F32).at[:batch].set(c)
    mod3 = _mod_call(c_pad, w_ada, b_ada).reshape(depth * MOD_ROWS, 1, 3 * d)
    tabs = _rope_tables(seq)
    w_all = _prep_call(w_in)
    norm_g3 = norm_g.reshape(depth, 1, d)
    wa, wb, wc, wo = (w.astype(BF16) for w in (w_br_a, w_br_b, w_br_c, w_out))
    h = x.reshape(batch * seq, d)
    for l in range(depth):
        zp = _inproj_call(h, mod3, norm_g3, w_all, tabs, l, seq, tm_in, rope=False)
        zr = _inproj_call(h, mod3, norm_g3, w_all, tabs, l, seq, tm_in, rope=True)
        ya = _dsa_call(zp, zr, batch, seq, tq, tq)
        yb = _swa_call(zp, zr, sinks, l, batch, seq, tq)
        yc = _sb_call(zp, batch, seq, tq, tq)
        h = _merge_call(ya, yb, yc, zp, h, mod3, wa, wb, wc, wo, final_g, l, seq, tm_mg,
                        final=(l == depth - 1))
    return h.reshape(batch, seq, d)
```

```python
import functools

import jax
import jax.numpy as jnp
import numpy as np
from jax import lax
from jax.experimental import pallas as pl
from jax.experimental.pallas import tpu as pltpu

F32 = jnp.float32
BF16 = jnp.bfloat16
I32 = jnp.int32
I16 = jnp.int16

D_MODEL = 1024
HEAD_DIM = 64
ROPE_THETA = 10000.0
EPS = 1e-6
N_HEADS = 8
N_PAIRS = N_HEADS // 2
IDX_HEADS = 4
TOPK_MAX = 256
B_KV_HEADS = 2
WINDOW = 128
MIX_W = N_HEADS * HEAD_DIM

LANES = 128
MOD_ROWS = 8
VMEM_LIMIT = 56 * 1024 * 1024
NEG = -1e30
LOG2E = 1.4426950408889634
RUN_FLOOR = -160.0
INT_MIN = -(2 ** 31)
I16_MIN = -(2 ** 15)
SUB32, SUB16 = 8, 16

C_MA, C_MB, C_MC = 0, 1024, 2048
C_AV, C_AG, C_BG, C_CQ, C_CK, C_CV, C_CG = 3072, 3584, 4096, 4608, 5120, 5632, 6144
C_AQ, C_AK, C_BQ = 6656, 7168, 7680
C_IQ = 8192
C_IK2, C_BK2, C_BK2S, C_BV2, C_BV2S, C_IW = 8448, 8576, 8704, 8832, 8960, 9088
NP = 9216
TN = 512
INPROJ_SPLIT = 4
NP_PLAIN = C_AQ
NP_ROPE = NP - NP_PLAIN
assert NP_PLAIN % TN == 0 and NP_ROPE % TN == 0 and C_BK2S == NP - TN

_SPLIT_NAMES = ("aq", "ak", "av", "ag", "iq", "ik", "iw", "bq", "bk", "bv", "bg",
                "cq", "ck", "cv", "cg", "ma", "mb", "mc")
_SPLIT_SIZES = (512, 512, 512, 512, 256, 64, 4, 512, 128, 128, 512,
                512, 512, 512, 512, 1024, 1024, 1024)


def _cparams(n_grid):
    return pltpu.CompilerParams(dimension_semantics=("arbitrary",) * n_grid,
                                vmem_limit_bytes=VMEM_LIMIT)


def _sigmoid(x):
    return 1.0 / (1.0 + jnp.exp(-x))


def _dot_nt(a, b):
    return lax.dot_general(a, b, (((1,), (1,)), ((), ())), preferred_element_type=F32)


def _lo_half(rows):
    return lax.broadcasted_iota(I32, (rows, LANES), 1) < HEAD_DIM


def _mod_kernel(c_ref, w_ref, b_ref, o_ref):
    c = c_ref[...]
    cs = c * _sigmoid(c)
    o_ref[0] = jnp.dot(cs, w_ref[0], preferred_element_type=F32,
                       precision=lax.Precision.HIGHEST) + b_ref[0]


def _mod_call(c_pad, w_ada, b_ada):
    depth, d, d3 = w_ada.shape
    rows = c_pad.shape[0]
    return pl.pallas_call(
        _mod_kernel,
        out_shape=jax.ShapeDtypeStruct((depth, rows, d3), F32),
        grid=(depth, d3 // d),
        in_specs=[
            pl.BlockSpec((rows, d), lambda l, j: (0, 0)),
            pl.BlockSpec((1, d, d), lambda l, j: (l, 0, j)),
            pl.BlockSpec((1, 1, d), lambda l, j: (l, 0, j)),
        ],
        out_specs=pl.BlockSpec((1, rows, d), lambda l, j: (l, 0, j)),
        compiler_params=_cparams(2),
        name="adaln_mod",
    )(c_pad, w_ada, b_ada.reshape(depth, 1, d3))


def _inproj_kernel(x_ref, shift_ref, scale_ref, g_ref, w_ref, *rest, rope):
    tab_ref = rest[0] if rope else None
    o_ref, u_sc = rest[-2:]
    j = pl.program_id(1)

    @pl.when(j == 0)
    def _():
        x = x_ref[...]
        y = x * lax.rsqrt(jnp.mean(x * x, axis=-1, keepdims=True) + EPS) * g_ref[0]
        u_sc[...] = (y * (1.0 + scale_ref[0]) + shift_ref[0]).astype(BF16)

    tm, tn = o_ref.shape
    sub_rows = tm // INPROJ_SPLIT
    if rope:
        lane = lax.broadcasted_iota(I32, (sub_rows, tn), 1)
        first = (lane & (HEAD_DIM - 1)) < HEAD_DIM // 2
        n_rope = jnp.where(j == pl.num_programs(1) - 1, LANES, tn)
    for r in range(INPROJ_SPLIT):
        rows = slice(r * sub_rows, (r + 1) * sub_rows)
        acc = jnp.dot(u_sc[rows, :], w_ref[0], preferred_element_type=F32)
        if rope:
            partner = jnp.where(first, pltpu.roll(acc, tn - HEAD_DIM // 2, 1),
                                pltpu.roll(acc, HEAD_DIM // 2, 1))
            roped = (acc * jnp.tile(tab_ref[0, rows, :], (1, tn // LANES))
                     + partner * jnp.tile(tab_ref[1, rows, :], (1, tn // LANES)))
            acc = jnp.where(lane < n_rope, roped, acc)
        o_ref[rows, :] = acc.astype(BF16)


def _inproj_call(h, mod3, norm_g, w, tabs, l, seq, tm, rope):
    rows, d = h.shape
    nqb = seq // tm
    col0, ncol = (NP_PLAIN // TN, NP_ROPE) if rope else (0, NP_PLAIN)
    in_specs = [
        pl.BlockSpec((tm, d), lambda i, j: (i, 0)),
        pl.BlockSpec((1, 1, d), lambda i, j: (l * MOD_ROWS + i // nqb, 0, 0)),
        pl.BlockSpec((1, 1, d), lambda i, j: (l * MOD_ROWS + i // nqb, 0, 1)),
        pl.BlockSpec((1, 1, d), lambda i, j: (l, 0, 0)),
        pl.BlockSpec((1, d, TN), lambda i, j: (l, 0, col0 + j)),
    ]
    args = [h, mod3, mod3, norm_g, w]
    if rope:
        in_specs.append(pl.BlockSpec((2, tm, LANES), lambda i, j: (0, i % nqb, 0)))
        args.append(tabs)
    return pl.pallas_call(
        functools.partial(_inproj_kernel, rope=rope),
        out_shape=jax.ShapeDtypeStruct((rows, ncol), BF16),
        grid=(rows // tm, ncol // TN),
        in_specs=in_specs,
        out_specs=pl.BlockSpec((tm, TN), lambda i, j: (i, j)),
        scratch_shapes=[pltpu.VMEM((tm, d), BF16)],
        compiler_params=_cparams(2),
        name="norm_inproj_rope" if rope else "norm_inproj",
    )(*args)


def _rope_tables(seq):
    half = HEAD_DIM // 2
    inv = 1.0 / (ROPE_THETA ** (jnp.arange(0, HEAD_DIM, 2, dtype=F32) / HEAD_DIM))
    ang = jnp.arange(seq, dtype=F32)[:, None] * inv[None, :]
    lane = np.arange(LANES)
    sign = np.where((lane % HEAD_DIM) < half, -1.0, 1.0).astype(np.float32)
    return jnp.stack([jnp.cos(ang)[:, lane % half], jnp.sin(ang)[:, lane % half] * sign[None, :]])


_SRC = dict(zip(_SPLIT_NAMES, np.cumsum((0,) + _SPLIT_SIZES[:-1]).tolist()))
_WHOLE = ((C_MA, "ma"), (C_MB, "mb"), (C_MC, "mc"), (C_AV, "av"), (C_AG, "ag"), (C_BG, "bg"),
          (C_CQ, "cq"), (C_CK, "ck"), (C_CV, "cv"), (C_CG, "cg"), (C_AQ, "aq"), (C_AK, "ak"),
          (C_BQ, "bq"), (C_IQ, "iq"), (C_BK2, "bk"), (C_BV2, "bv"))
_HALVES = ((C_IK2, _SRC["ik"], _SRC["ik"]),
           (C_BK2S, _SRC["bk"] + HEAD_DIM, _SRC["bk"]),
           (C_BV2S, _SRC["bv"] + HEAD_DIM, _SRC["bv"]))


def _prep_kernel(w_ref, o_ref):
    rows = w_ref.shape[1]
    lane = lax.broadcasted_iota(I32, (rows, LANES), 1)
    sizes = dict(zip(_SPLIT_NAMES, _SPLIT_SIZES))
    for dst, name in _WHOLE:
        o_ref[0, :, dst:dst + sizes[name]] = w_ref[0, :, _SRC[name]:_SRC[name] + sizes[name]].astype(BF16)
    for dst, lo, hi in _HALVES:
        low = w_ref[0, :, lo:lo + LANES]
        high = w_ref[0, :, hi - HEAD_DIM:hi - HEAD_DIM + LANES]
        o_ref[0, :, dst:dst + LANES] = jnp.where(lane < HEAD_DIM, low, high).astype(BF16)
    iw = w_ref[0, :, _SRC["iw"]:_SRC["iw"] + LANES]
    o_ref[0, :, C_IW:C_IW + LANES] = jnp.where(lane < sizes["iw"], iw, 0.0).astype(BF16)


def _prep_call(w):
    depth, d, n = w.shape
    rows = 128
    return pl.pallas_call(
        _prep_kernel,
        out_shape=jax.ShapeDtypeStruct((depth, d, NP), BF16),
        grid=(depth, d // rows),
        in_specs=[pl.BlockSpec((1, rows, n), lambda l, i: (l, i, 0))],
        out_specs=pl.BlockSpec((1, rows, NP), lambda l, i: (l, i, 0)),
        compiler_params=_cparams(2),
        name="inproj_weight_layout",
    )(w)


def _gated_store(o_ref, g_ref, p, val):
    sl = slice(p * LANES, (p + 1) * LANES)
    g = g_ref[:, sl].astype(F32)
    o_ref[:, sl] = (val * (g * _sigmoid(g))).astype(o_ref.dtype)


def _dsa_kernel(q_ref, k_ref, v_ref, g_ref, iq_ref, ik_ref, iw_ref, o_ref,
                keys_sc, khi_sc, klo_sc, qm_sc, iqm_sc, w_sc, m_sc, l_sc, acc_sc, *, tq, ck, topk):
    i = pl.program_id(1)
    ndiag = tq // ck
    nck = (i + 1) * ndiag
    nfull = nck - ndiag
    lo_half = _lo_half(tq)
    kf = jnp.float32(topk)

    for p in range(N_PAIRS):
        qp = q_ref[:, p * LANES:(p + 1) * LANES].astype(F32) * (HEAD_DIM ** -0.5)
        qm_sc[2 * p] = jnp.where(lo_half, qp, 0.0).astype(BF16)
        qm_sc[2 * p + 1] = jnp.where(lo_half, 0.0, qp).astype(BF16)
    for p in range(IDX_HEADS // 2):
        t = iq_ref[:, p * LANES:(p + 1) * LANES].astype(F32)
        iqm_sc[2 * p] = jnp.where(lo_half, t, 0.0).astype(BF16)
        iqm_sc[2 * p + 1] = jnp.where(lo_half, 0.0, t).astype(BF16)
    w_sc[...] = iw_ref[...].astype(F32).T[:SUB32] * ((IDX_HEADS ** -0.5) * (HEAD_DIM ** -0.5))

    q_pos = i * tq + lax.broadcasted_iota(I32, (ck, tq), 1)
    k_off = lax.broadcasted_iota(I32, (ck, tq), 0)

    def score_chunks(cs, masked):
        k0s = [pl.multiple_of(c * ck, ck) for c in cs]
        ds = [[_dot_nt(ik_ref[pl.ds(k0, ck), :], iqm_sc[h]) for h in range(IDX_HEADS)] for k0 in k0s]
        for c, k0, d in zip(cs, k0s, ds):
            sc = w_sc[0:1, :] * jnp.maximum(d[0], 0.0)
            for h in range(1, IDX_HEADS):
                sc = sc + w_sc[h:h + 1, :] * jnp.maximum(d[h], 0.0)
            bits = pltpu.bitcast(sc, I32)
            key = bits ^ ((bits >> 31) & 0x7FFFFFFF)
            if masked:
                key = jnp.where(k0 + k_off <= q_pos, key, INT_MIN)
            keys_sc[c] = key
            khi_sc[c] = (key >> 16).astype(I16)

    def score_pair(c2, carry):
        score_chunks([2 * c2, 2 * c2 + 1], False)
        return carry

    lax.fori_loop(0, nfull // 2, score_pair, 0)

    @pl.when(nfull % 2 == 1)
    def _():
        score_chunks([nfull - 1], False)

    score_chunks([nfull + t for t in range(ndiag)], True)

    npair = (nck + 1) // 2

    @pl.when(nck % 2 == 1)
    def _():
        khi_sc[nck] = jnp.full((ck, tq), I16_MIN, I16)
        klo_sc[nck] = jnp.full((ck, tq), I16_MIN, I16)

    def count16(ref, thr, strict=False):
        def body(c2, parts):
            new = []
            for j, part in enumerate(parts):
                x = ref[2 * c2 + j]
                ones = jnp.where((x > thr) if strict else (x >= thr), jnp.bfloat16(1), jnp.bfloat16(0))
                for r in range(ck // SUB16):
                    part = part + ones[r * SUB16:(r + 1) * SUB16]
                new.append(part)
            return tuple(new)

        zero = jnp.zeros((SUB16, tq), BF16)
        parts = lax.fori_loop(0, npair, body, (zero, zero))
        return jnp.sum(parts[0].astype(F32) + parts[1].astype(F32), axis=0, keepdims=True)

    def bisect16(ref, want):
        def step(it, res):
            cand = res + jnp.left_shift(jnp.int32(1), 15 - it)
            return jnp.where(count16(ref, cand.astype(I16)) >= want, cand, res)

        return lax.fori_loop(0, 16, step, jnp.full((1, tq), I16_MIN, I32))

    hi = bisect16(khi_sc, kf)
    hi16 = hi.astype(I16)
    want_lo = kf - count16(khi_sc, hi16, strict=True)

    def low_half(c, carry):
        lo = (keys_sc[c] ^ 0x8000).astype(I16)
        klo_sc[c] = jnp.where(khi_sc[c] == hi16, lo, jnp.int16(I16_MIN))
        return carry

    lax.fori_loop(0, nck, low_half, 0)
    lo = bisect16(klo_sc, want_lo)
    res = (hi << 16) | ((lo ^ 0x8000) & 0xFFFF)
    tau = jnp.maximum(res, INT_MIN + 1)

    def count_ge(thr):
        def body(c, part):
            ones = jnp.where(keys_sc[c] >= thr, 1.0, 0.0)
            return part + jnp.sum(ones.reshape(ck // SUB32, SUB32, tq), axis=0)

        part = lax.fori_loop(0, nck, body, jnp.zeros((SUB32, tq), F32))
        return jnp.sum(part, axis=0, keepdims=True)

    n_ge = count_ge(tau)

    @pl.when(jnp.max(n_ge) > kf)
    def _():
        need = kf - count_ge(tau + 1)
        incl = (lax.broadcasted_iota(I32, (ck, ck), 1)
                <= lax.broadcasted_iota(I32, (ck, ck), 0)).astype(F32).astype(BF16)

        def body(c, seen):
            kc = keys_sc[c]
            eq = jnp.where(kc == tau, 1.0, 0.0)
            rank = jnp.dot(incl, eq.astype(BF16), preferred_element_type=F32) + seen
            drop = eq * jnp.where(rank > need, 1.0, 0.0)
            keys_sc[c] = jnp.where(drop > 0.5, tau - 1, kc)
            return seen + jnp.sum(eq, axis=0, keepdims=True)

        lax.fori_loop(0, nck, body, jnp.zeros((1, tq), F32))

    m_sc[...] = jnp.full(m_sc.shape, NEG, F32)
    l_sc[...] = jnp.zeros(l_sc.shape, F32)
    acc_sc[...] = jnp.zeros(acc_sc.shape, F32)
    ones_rows = jnp.ones((SUB16, ck), BF16)

    def attend_chunks(cs):
        k0s = [pl.multiple_of(c * ck, ck) for c in cs]
        biases = [jnp.where(keys_sc[c] >= tau, 0.0, NEG) for c in cs]
        ss, ps, alphas, vts = {}, {}, {}, {}

        def scores(j, h):
            sl = slice((h // 2) * LANES, (h // 2 + 1) * LANES)
            ss[j, h] = _dot_nt(k_ref[pl.ds(k0s[j], ck), sl], qm_sc[h]) + biases[j]
            if h % 2 == 0:
                vts[j, h // 2] = v_ref[pl.ds(k0s[j], ck), sl].astype(F32).T.astype(BF16)

        def softmax(j, h):
            s = ss.pop((j, h))
            m_prev = m_sc[h:h + 1, :]
            m_next = jnp.maximum(m_prev, jnp.max(s, axis=0, keepdims=True))
            ps[j, h] = jnp.exp(s - m_next).astype(BF16)
            alphas[j, h] = jnp.exp(m_prev - m_next)
            m_sc[h:h + 1, :] = m_next

        def values(j, h):
            e = h % 2
            vt = jnp.concatenate([vts[j, h // 2][e * HEAD_DIM:(e + 1) * HEAD_DIM], ones_rows], axis=0)
            pv = jnp.dot(vt, ps.pop((j, h)), preferred_element_type=F32)
            alpha = alphas.pop((j, h))
            l_sc[h:h + 1, :] = alpha * l_sc[h:h + 1, :] + pv[HEAD_DIM:HEAD_DIM + 1, :]
            acc_sc[h] = alpha * acc_sc[h] + pv[:HEAD_DIM, :]

        work = [(j, h) for j in range(len(cs)) for h in range(N_HEADS)]
        stages = (scores, softmax, values)
        for step in range(len(work) + len(stages) - 1):
            for s, stage in enumerate(stages):
                if 0 <= step - s < len(work):
                    stage(*work[step - s])

    def attend_pair(c2, carry):
        attend_chunks([2 * c2, 2 * c2 + 1])
        return carry

    lax.fori_loop(0, nck // 2, attend_pair, 0)

    @pl.when(nck % 2 == 1)
    def _():
        attend_chunks([nck - 1])

    for p in range(N_PAIRS):
        he, ho = 2 * p, 2 * p + 1
        out_t = jnp.concatenate([acc_sc[he] / l_sc[he:he + 1, :],
                                 acc_sc[ho] / l_sc[ho:ho + 1, :]], axis=0)
        _gated_store(o_ref, g_ref, p, out_t.T)


def _dsa_call(zp, zr, batch, seq, tq, ck):
    topk = min(TOPK_MAX, seq // 4)
    nq = seq // tq
    assert seq // SUB16 <= 256
    n_even = 2 * pl.cdiv(seq // ck, 2)
    kern = functools.partial(_dsa_kernel, tq=tq, ck=ck, topk=topk)
    return pl.pallas_call(
        kern,
        out_shape=jax.ShapeDtypeStruct((batch * seq, MIX_W), BF16),
        grid=(batch, nq),
        in_specs=[
            pl.BlockSpec((tq, MIX_W), lambda b, i: (b * nq + i, (C_AQ - NP_PLAIN) // MIX_W)),
            pl.BlockSpec((seq, MIX_W), lambda b, i: (b, (C_AK - NP_PLAIN) // MIX_W)),
            pl.BlockSpec((seq, MIX_W), lambda b, i: (b, C_AV // MIX_W)),
            pl.BlockSpec((tq, MIX_W), lambda b, i: (b * nq + i, C_AG // MIX_W)),
            pl.BlockSpec((tq, 2 * LANES), lambda b, i: (b * nq + i, (C_IQ - NP_PLAIN) // (2 * LANES))),
            pl.BlockSpec((seq, LANES), lambda b, i: (b, (C_IK2 - NP_PLAIN) // LANES)),
            pl.BlockSpec((tq, LANES), lambda b, i: (b * nq + i, (C_IW - NP_PLAIN) // LANES)),
        ],
        out_specs=pl.BlockSpec((tq, MIX_W), lambda b, i: (b * nq + i, 0)),
        scratch_shapes=[
            pltpu.VMEM((seq // ck, ck, tq), I32),
            pltpu.VMEM((n_even, ck, tq), I16),
            pltpu.VMEM((n_even, ck, tq), I16),
            pltpu.VMEM((N_HEADS, tq, LANES), BF16),
            pltpu.VMEM((IDX_HEADS, tq, LANES), BF16),
            pltpu.VMEM((SUB32, tq), F32),
            pltpu.VMEM((N_HEADS, tq), F32),
            pltpu.VMEM((N_HEADS, tq), F32),
            pltpu.VMEM((N_HEADS, HEAD_DIM, tq), F32),
        ],
        compiler_params=_cparams(2),
        name="dsa_attention",
    )(zr, zr, zp, zp, zr, zr, zr)


def _swa_kernel(sinks_ref, q_ref, k2_ref, k2s_ref, v2_ref, v2s_ref, g_ref, o_ref, *, tq, layer):
    i = pl.program_id(1)
    span = tq + WINDOW
    kstart = pl.multiple_of(jnp.maximum(i * tq - WINDOW, 0), WINDOW)
    lo_half = _lo_half(tq)
    row = i * tq + lax.broadcasted_iota(I32, (tq, span), 0)
    col = kstart + lax.broadcasted_iota(I32, (tq, span), 1)
    bias = jnp.where((col <= row) & (col > row - WINDOW), 0.0, NEG)
    kv = {False: (k2s_ref[pl.ds(kstart, span), :], v2s_ref[pl.ds(kstart, span), :]),
          True: (k2_ref[pl.ds(kstart, span), :], v2_ref[pl.ds(kstart, span), :])}
    grp = N_HEADS // B_KV_HEADS
    ss, ps, dens, outs = {}, {}, {}, {}

    def scores(h):
        p, e = divmod(h, 2)
        qp = q_ref[:, p * LANES:(p + 1) * LANES].astype(F32) * (HEAD_DIM ** -0.5)
        qm = jnp.where(lo_half if e == 0 else ~lo_half, qp, 0.0).astype(BF16)
        ss[h] = _dot_nt(qm, kv[(h // grp) == e][0]) + bias

    def softmax(h):
        s = ss.pop(h)
        sink = sinks_ref[layer, h]
        m = jnp.maximum(jnp.max(s, axis=1, keepdims=True), sink)
        pe = jnp.exp(s - m)
        dens[h] = jnp.sum(pe, axis=1, keepdims=True) + jnp.exp(sink - m)
        ps[h] = pe.astype(BF16)

    def values(h):
        vv = kv[(h // grp) == (h % 2)][1]
        outs[h] = jnp.dot(ps.pop(h), vv, preferred_element_type=F32) / dens.pop(h)
        if h % 2 == 1:
            _gated_store(o_ref, g_ref, h // 2, jnp.where(lo_half, outs.pop(h - 1), outs.pop(h)))

    stages = (scores, softmax, values)
    for step in range(N_HEADS + len(stages) - 1):
        for s_i, stage in enumerate(stages):
            if 0 <= step - s_i < N_HEADS:
                stage(step - s_i)


def _swa_call(zp, zr, sinks, layer, batch, seq, tq):
    nq = seq // tq
    kv_spec = lambda c: pl.BlockSpec((seq, LANES), lambda b, i: (b, (c - NP_PLAIN) // LANES))
    return pl.pallas_call(
        functools.partial(_swa_kernel, tq=tq, layer=layer),
        out_shape=jax.ShapeDtypeStruct((batch * seq, MIX_W), BF16),
        grid=(batch, nq),
        in_specs=[
            pl.BlockSpec(memory_space=pltpu.SMEM),
            pl.BlockSpec((tq, MIX_W), lambda b, i: (b * nq + i, (C_BQ - NP_PLAIN) // MIX_W)),
            kv_spec(C_BK2), kv_spec(C_BK2S), kv_spec(C_BV2), kv_spec(C_BV2S),
            pl.BlockSpec((tq, MIX_W), lambda b, i: (b * nq + i, C_BG // MIX_W)),
        ],
        out_specs=pl.BlockSpec((tq, MIX_W), lambda b, i: (b * nq + i, 0)),
        compiler_params=_cparams(2),
        name="swa_sinks_attention",
    )(sinks, zr, zr, zr, zr, zr, zp)


def _sb_kernel(q_ref, k_ref, v_ref, g_ref, o_ref, qm_sc, run_sc, acc_sc, *, tq, ck):
    i = pl.program_id(1)
    ndiag = tq // ck
    nck = (i + 1) * ndiag
    rep = ck // LANES
    lo_half = _lo_half(tq)
    row = i * tq + lax.broadcasted_iota(I32, (tq, ck), 0)
    col_l = lax.broadcasted_iota(I32, (tq, ck), 1)
    j_idx = lax.broadcasted_iota(I32, (2 * ck, ck), 0)
    j_idx = jnp.where(j_idx >= ck, j_idx - ck, j_idx)
    neg_later2 = jnp.where(j_idx > lax.broadcasted_iota(I32, (2 * ck, ck), 1),
                           -1.0, 0.0).astype(BF16)

    for p in range(N_PAIRS):
        qp = q_ref[:, p * LANES:(p + 1) * LANES].astype(F32) * (HEAD_DIM ** -0.5)
        qm_sc[2 * p] = jnp.where(lo_half, qp, 0.0).astype(BF16)
        qm_sc[2 * p + 1] = jnp.where(lo_half, 0.0, qp).astype(BF16)
    run_sc[...] = jnp.zeros(run_sc.shape, F32)
    acc_sc[...] = jnp.zeros(acc_sc.shape, F32)

    def chunk(t, masked):
        k0 = pl.multiple_of((nck - 1 - t) * ck, ck)
        strict = (k0 + col_l < row) if masked else None
        zs, nlms, lbs, afters = {}, {}, {}, {}

        def scores(h):
            kp = k_ref[pl.ds(k0, ck), (h // 2) * LANES:(h // 2 + 1) * LANES]
            zs[h] = _dot_nt(qm_sc[h], kp) * LOG2E

        def logs(h):
            z2 = zs.pop(h)
            nlm = jnp.maximum(z2, 0.0) + jnp.log2(1.0 + jnp.exp2(-jnp.abs(z2)))
            lbs[h] = z2 - nlm
            nlms[h] = jnp.where(strict, nlm, 0.0) if masked else nlm

        def cumsum(h):
            hi = nlms[h].astype(BF16)
            lo = (nlms[h] - hi.astype(F32)).astype(BF16)
            afters[h] = jnp.dot(jnp.concatenate([hi, lo], axis=1), neg_later2,
                                preferred_element_type=F32)

        def weights(h):
            run = run_sc[h]
            a = jnp.exp2(lbs.pop(h) + afters.pop(h) + jnp.tile(run, (1, rep)))
            if masked:
                a = jnp.where(strict, a, 0.0)
            vp = v_ref[pl.ds(k0, ck), (h // 2) * LANES:(h // 2 + 1) * LANES]
            acc_sc[h] += jnp.dot(a.astype(BF16), vp, preferred_element_type=F32)
            run_sc[h] = run - jnp.sum(nlms.pop(h), axis=1, keepdims=True)

        stages = (scores, logs, cumsum, weights)
        for step in range(N_HEADS + len(stages) - 1):
            for s, stage in enumerate(stages):
                if 0 <= step - s < N_HEADS:
                    stage(step - s)

    for t in range(ndiag):
        chunk(t, True)

    def live():
        return jnp.max(run_sc[...]) > RUN_FLOOR

    def body(carry):
        t, _ = carry
        chunk(t, False)
        return t + 1, live()

    lax.while_loop(lambda carry: jnp.logical_and(carry[0] < nck, carry[1]), body,
                   (jnp.int32(ndiag), live()))

    for p in range(N_PAIRS):
        _gated_store(o_ref, g_ref, p, jnp.where(lo_half, acc_sc[2 * p], acc_sc[2 * p + 1]))


def _sb_call(z, batch, seq, tq, ck):
    nq = seq // tq
    state = pltpu.VMEM((N_HEADS, tq, LANES), F32)
    return pl.pallas_call(
        functools.partial(_sb_kernel, tq=tq, ck=ck),
        out_shape=jax.ShapeDtypeStruct((batch * seq, MIX_W), BF16),
        grid=(batch, nq),
        in_specs=[
            pl.BlockSpec((tq, MIX_W), lambda b, i: (b * nq + i, C_CQ // MIX_W)),
            pl.BlockSpec((seq, MIX_W), lambda b, i: (b, C_CK // MIX_W)),
            pl.BlockSpec((seq, MIX_W), lambda b, i: (b, C_CV // MIX_W)),
            pl.BlockSpec((tq, MIX_W), lambda b, i: (b * nq + i, C_CG // MIX_W)),
        ],
        out_specs=pl.BlockSpec((tq, MIX_W), lambda b, i: (b * nq + i, 0)),
        scratch_shapes=[pltpu.VMEM((N_HEADS, tq, LANES), BF16), state, state],
        compiler_params=_cparams(2),
        name="stick_breaking_attention",
    )(z, z, z, z)


def _merge_kernel(ya_ref, yb_ref, yc_ref, ma_ref, mb_ref, mc_ref, x_ref, gate_ref,
                  wa_ref, wb_ref, wc_ref, wo_ref, fg_ref, o_ref, *, final):
    def branch(y_ref, m_ref, w_ref):
        pr = jnp.dot(y_ref[...], w_ref[0], preferred_element_type=F32)
        return _sigmoid(m_ref[...].astype(F32)) * pr

    merged = (branch(ya_ref, ma_ref, wa_ref) + branch(yb_ref, mb_ref, wb_ref)
              + branch(yc_ref, mc_ref, wc_ref))
    out = x_ref[...] + gate_ref[0] * jnp.dot(merged.astype(BF16), wo_ref[0],
                                             preferred_element_type=F32)
    if final:
        out = out * lax.rsqrt(jnp.mean(out * out, axis=-1, keepdims=True) + EPS) * fg_ref[...]
    o_ref[...] = out


def _merge_call(ya, yb, yc, z, h, mod3, wa, wb, wc, wo, fg, l, seq, tm, final):
    rows, d = h.shape
    nqb = seq // tm
    y_spec = pl.BlockSpec((tm, MIX_W), lambda i: (i, 0))
    m_spec = lambda c: pl.BlockSpec((tm, d), lambda i: (i, c // d))
    w_spec = pl.BlockSpec((1, MIX_W, d), lambda i: (l, 0, 0))
    return pl.pallas_call(
        functools.partial(_merge_kernel, final=final),
        out_shape=jax.ShapeDtypeStruct((rows, d), F32),
        grid=(rows // tm,),
        in_specs=[
            y_spec, y_spec, y_spec, m_spec(C_MA), m_spec(C_MB), m_spec(C_MC),
            pl.BlockSpec((tm, d), lambda i: (i, 0)),
            pl.BlockSpec((1, 1, d), lambda i: (l * MOD_ROWS + i // nqb, 0, 2)),
            w_spec, w_spec, w_spec,
            pl.BlockSpec((1, d, d), lambda i: (l, 0, 0)),
            pl.BlockSpec((1, d), lambda i: (0, 0)),
        ],
        out_specs=pl.BlockSpec((tm, d), lambda i: (i, 0)),
        compiler_params=_cparams(1),
        name="merge_outproj_residual",
    )(ya, yb, yc, z, z, z, h, mod3, wa, wb, wc, wo, fg.reshape(1, d))


def _tiles(seq):
    big = 512 if seq % 512 == 0 else 256
    return min(1024, seq), big, 256, big


def kernel(x, c, norm_g, w_ada, b_ada, w_in, sinks, w_br_a, w_br_b, w_br_c, w_out, final_g):
    batch, seq, d = x.shape
    depth = w_in.shape[0]
    assert d == D_MODEL and seq % 256 == 0 and seq >= 256 + WINDOW and batch <= MOD_ROWS
    tm_in, tq_a, tq, tm_mg = _tiles(seq)

    c_pad = jnp.zeros((MOD_ROWS, d), F32).at[:batch].set(c)
    mod3 = _mod_call(c_pad, w_ada, b_ada).reshape(depth * MOD_ROWS, 1, 3 * d)
    tabs = _rope_tables(seq)
    w_all = _prep_call(w_in)
    norm_g3 = norm_g.reshape(depth, 1, d)
    wa, wb, wc, wo = (w.astype(BF16) for w in (w_br_a, w_br_b, w_br_c, w_out))
    h = x.reshape(batch * seq, d)
    for l in range(depth):
        zp = _inproj_call(h, mod3, norm_g3, w_all, tabs, l, seq, tm_in, rope=False)
        zr = _inproj_call(h, mod3, norm_g3, w_all, tabs, l, seq, tm_in, rope=True)
        ya = _dsa_call(zp, zr, batch, seq, tq_a, tq_a)
        yb = _swa_call(zp, zr, sinks, l, batch, seq, tq)
        yc = _sb_call(zp, batch, seq, tq, tq)
        h = _merge_call(ya, yb, yc, zp, h, mod3, wa, wb, wc, wo, final_g, l, seq, tm_mg,
                        final=(l == depth - 1))
    return h.reshape(batch, seq, d)
```

```python
import functools

import jax
import jax.numpy as jnp
import numpy as np
from jax import lax
from jax.experimental import pallas as pl
from jax.experimental.pallas import tpu as pltpu

F32 = jnp.float32
BF16 = jnp.bfloat16
I32 = jnp.int32
I16 = jnp.int16

D_MODEL = 1024
HEAD_DIM = 64
ROPE_THETA = 10000.0
EPS = 1e-6
N_HEADS = 8
N_PAIRS = N_HEADS // 2
IDX_HEADS = 4
TOPK_MAX = 256
B_KV_HEADS = 2
WINDOW = 128
MIX_W = N_HEADS * HEAD_DIM

LANES = 128
MOD_ROWS = 8
VMEM_LIMIT = 56 * 1024 * 1024
NEG = -1e30
LOG2E = 1.4426950408889634
RUN_FLOOR = -160.0
INT_MIN = -(2 ** 31)
I16_MIN = -(2 ** 15)
SUB32, SUB16 = 8, 16

C_MA, C_MB, C_MC = 0, 1024, 2048
C_AV, C_AG, C_BG, C_CQ, C_CK, C_CV, C_CG = 3072, 3584, 4096, 4608, 5120, 5632, 6144
C_AQ, C_AK, C_BQ = 6656, 7168, 7680
C_IQ = 8192
C_IK2, C_BK2, C_BK2S, C_BV2, C_BV2S, C_IW = 8448, 8576, 8704, 8832, 8960, 9088
NP = 9216
TN = 512
INPROJ_SPLIT = 4
NP_PLAIN = C_AQ
NP_ROPE = NP - NP_PLAIN
assert NP_PLAIN % TN == 0 and NP_ROPE % TN == 0 and C_BK2S == NP - TN

_SPLIT_NAMES = ("aq", "ak", "av", "ag", "iq", "ik", "iw", "bq", "bk", "bv", "bg",
                "cq", "ck", "cv", "cg", "ma", "mb", "mc")
_SPLIT_SIZES = (512, 512, 512, 512, 256, 64, 4, 512, 128, 128, 512,
                512, 512, 512, 512, 1024, 1024, 1024)


def _cparams(n_grid):
    return pltpu.CompilerParams(dimension_semantics=("arbitrary",) * n_grid,
                                vmem_limit_bytes=VMEM_LIMIT)


def _sigmoid(x):
    return 1.0 / (1.0 + jnp.exp(-x))


def _dot_nt(a, b):
    return lax.dot_general(a, b, (((1,), (1,)), ((), ())), preferred_element_type=F32)


def _lo_half(rows):
    return lax.broadcasted_iota(I32, (rows, LANES), 1) < HEAD_DIM


def _mod_kernel(c_ref, w_ref, b_ref, o_ref):
    c = c_ref[...]
    cs = c * _sigmoid(c)
    o_ref[0] = jnp.dot(cs, w_ref[0], preferred_element_type=F32,
                       precision=lax.Precision.HIGHEST) + b_ref[0]


def _mod_call(c_pad, w_ada, b_ada):
    depth, d, d3 = w_ada.shape
    rows = c_pad.shape[0]
    return pl.pallas_call(
        _mod_kernel,
        out_shape=jax.ShapeDtypeStruct((depth, rows, d3), F32),
        grid=(depth, d3 // d),
        in_specs=[
            pl.BlockSpec((rows, d), lambda l, j: (0, 0)),
            pl.BlockSpec((1, d, d), lambda l, j: (l, 0, j)),
            pl.BlockSpec((1, 1, d), lambda l, j: (l, 0, j)),
        ],
        out_specs=pl.BlockSpec((1, rows, d), lambda l, j: (l, 0, j)),
        compiler_params=_cparams(2),
        name="adaln_mod",
    )(c_pad, w_ada, b_ada.reshape(depth, 1, d3))


def _inproj_kernel(x_ref, shift_ref, scale_ref, g_ref, w_ref, *rest, rope):
    tab_ref = rest[0] if rope else None
    o_ref, u_sc = rest[-2:]
    j = pl.program_id(1)

    @pl.when(j == 0)
    def _():
        x = x_ref[...]
        y = x * lax.rsqrt(jnp.mean(x * x, axis=-1, keepdims=True) + EPS) * g_ref[0]
        u_sc[...] = (y * (1.0 + scale_ref[0]) + shift_ref[0]).astype(BF16)

    tm, tn = o_ref.shape
    sub_rows = tm // INPROJ_SPLIT
    if rope:
        lane = lax.broadcasted_iota(I32, (sub_rows, tn), 1)
        first = (lane & (HEAD_DIM - 1)) < HEAD_DIM // 2
        n_rope = jnp.where(j == pl.num_programs(1) - 1, LANES, tn)
    for r in range(INPROJ_SPLIT):
        rows = slice(r * sub_rows, (r + 1) * sub_rows)
        acc = jnp.dot(u_sc[rows, :], w_ref[0], preferred_element_type=F32)
        if rope:
            partner = jnp.where(first, pltpu.roll(acc, tn - HEAD_DIM // 2, 1),
                                pltpu.roll(acc, HEAD_DIM // 2, 1))
            roped = (acc * jnp.tile(tab_ref[0, rows, :], (1, tn // LANES))
                     + partner * jnp.tile(tab_ref[1, rows, :], (1, tn // LANES)))
            acc = jnp.where(lane < n_rope, roped, acc)
        o_ref[rows, :] = acc.astype(BF16)


def _inproj_call(h, mod3, norm_g, w, tabs, l, seq, tm, rope):
    rows, d = h.shape
    nqb = seq // tm
    col0, ncol = (NP_PLAIN // TN, NP_ROPE) if rope else (0, NP_PLAIN)
    in_specs = [
        pl.BlockSpec((tm, d), lambda i, j: (i, 0)),
        pl.BlockSpec((1, 1, d), lambda i, j: (l * MOD_ROWS + i // nqb, 0, 0)),
        pl.BlockSpec((1, 1, d), lambda i, j: (l * MOD_ROWS + i // nqb, 0, 1)),
        pl.BlockSpec((1, 1, d), lambda i, j: (l, 0, 0)),
        pl.BlockSpec((1, d, TN), lambda i, j: (l, 0, col0 + j)),
    ]
    args = [h, mod3, mod3, norm_g, w]
    if rope:
        in_specs.append(pl.BlockSpec((2, tm, LANES), lambda i, j: (0, i % nqb, 0)))
        args.append(tabs)
    return pl.pallas_call(
        functools.partial(_inproj_kernel, rope=rope),
        out_shape=jax.ShapeDtypeStruct((rows, ncol), BF16),
        grid=(rows // tm, ncol // TN),
        in_specs=in_specs,
        out_specs=pl.BlockSpec((tm, TN), lambda i, j: (i, j)),
        scratch_shapes=[pltpu.VMEM((tm, d), BF16)],
        compiler_params=_cparams(2),
        name="norm_inproj_rope" if rope else "norm_inproj",
    )(*args)


def _rope_tables(seq):
    half = HEAD_DIM // 2
    inv = 1.0 / (ROPE_THETA ** (jnp.arange(0, HEAD_DIM, 2, dtype=F32) / HEAD_DIM))
    ang = jnp.arange(seq, dtype=F32)[:, None] * inv[None, :]
    lane = np.arange(LANES)
    sign = np.where((lane % HEAD_DIM) < half, -1.0, 1.0).astype(np.float32)
    return jnp.stack([jnp.cos(ang)[:, lane % half], jnp.sin(ang)[:, lane % half] * sign[None, :]])


_SRC = dict(zip(_SPLIT_NAMES, np.cumsum((0,) + _SPLIT_SIZES[:-1]).tolist()))
_WHOLE = ((C_MA, "ma"), (C_MB, "mb"), (C_MC, "mc"), (C_AV, "av"), (C_AG, "ag"), (C_BG, "bg"),
          (C_CQ, "cq"), (C_CK, "ck"), (C_CV, "cv"), (C_CG, "cg"), (C_AQ, "aq"), (C_AK, "ak"),
          (C_BQ, "bq"), (C_IQ, "iq"), (C_BK2, "bk"), (C_BV2, "bv"))
_HALVES = ((C_IK2, _SRC["ik"], _SRC["ik"]),
           (C_BK2S, _SRC["bk"] + HEAD_DIM, _SRC["bk"]),
           (C_BV2S, _SRC["bv"] + HEAD_DIM, _SRC["bv"]))


def _prep_kernel(w_ref, o_ref):
    rows = w_ref.shape[1]
    lane = lax.broadcasted_iota(I32, (rows, LANES), 1)
    sizes = dict(zip(_SPLIT_NAMES, _SPLIT_SIZES))
    for dst, name in _WHOLE:
        o_ref[0, :, dst:dst + sizes[name]] = w_ref[0, :, _SRC[name]:_SRC[name] + sizes[name]].astype(BF16)
    for dst, lo, hi in _HALVES:
        low = w_ref[0, :, lo:lo + LANES]
        high = w_ref[0, :, hi - HEAD_DIM:hi - HEAD_DIM + LANES]
        o_ref[0, :, dst:dst + LANES] = jnp.where(lane < HEAD_DIM, low, high).astype(BF16)
    iw = w_ref[0, :, _SRC["iw"]:_SRC["iw"] + LANES]
    o_ref[0, :, C_IW:C_IW + LANES] = jnp.where(lane < sizes["iw"], iw, 0.0).astype(BF16)


def _prep_call(w):
    depth, d, n = w.shape
    rows = 128
    return pl.pallas_call(
        _prep_kernel,
        out_shape=jax.ShapeDtypeStruct((depth, d, NP), BF16),
        grid=(depth, d // rows),
        in_specs=[pl.BlockSpec((1, rows, n), lambda l, i: (l, i, 0))],
        out_specs=pl.BlockSpec((1, rows, NP), lambda l, i: (l, i, 0)),
        compiler_params=_cparams(2),
        name="inproj_weight_layout",
    )(w)


def _gated_store(o_ref, g_ref, p, val):
    sl = slice(p * LANES, (p + 1) * LANES)
    g = g_ref[:, sl].astype(F32)
    o_ref[:, sl] = (val * (g * _sigmoid(g))).astype(o_ref.dtype)


def _dsa_kernel(q_ref, k_ref, v_ref, g_ref, iq_ref, ik_ref, iw_ref, o_ref,
                s_sc, khi_sc, klo_sc, qm_sc, iqm_sc, w_sc, m_sc, l_sc, acc_sc, *, tq, ck, topk):
    i = pl.program_id(1)
    ndiag = tq // ck
    nck = (i + 1) * ndiag
    nfull = nck - ndiag
    lo_half = _lo_half(tq)
    kf = jnp.float32(topk)

    for p in range(N_PAIRS):
        qp = q_ref[:, p * LANES:(p + 1) * LANES].astype(F32) * (HEAD_DIM ** -0.5)
        qm_sc[2 * p] = jnp.where(lo_half, qp, 0.0).astype(BF16)
        qm_sc[2 * p + 1] = jnp.where(lo_half, 0.0, qp).astype(BF16)
    for p in range(IDX_HEADS // 2):
        t = iq_ref[:, p * LANES:(p + 1) * LANES].astype(F32)
        iqm_sc[2 * p] = jnp.where(lo_half, t, 0.0).astype(BF16)
        iqm_sc[2 * p + 1] = jnp.where(lo_half, 0.0, t).astype(BF16)
    w_sc[...] = iw_ref[...].astype(F32).T[:SUB32] * ((IDX_HEADS ** -0.5) * (HEAD_DIM ** -0.5))

    q_pos = i * tq + lax.broadcasted_iota(I32, (ck, tq), 1)
    k_off = lax.broadcasted_iota(I32, (ck, tq), 0)

    def sort_key(x):
        bits = pltpu.bitcast(x, I32)
        return bits ^ ((bits >> 31) & 0x7FFFFFFF)

    def score_chunks(cs, masked):
        k0s = [pl.multiple_of(c * ck, ck) for c in cs]
        ds = [[_dot_nt(ik_ref[pl.ds(k0, ck), :], iqm_sc[h]) for h in range(IDX_HEADS)] for k0 in k0s]
        for c, k0, d in zip(cs, k0s, ds):
            sc = w_sc[0:1, :] * jnp.maximum(d[0], 0.0)
            for h in range(1, IDX_HEADS):
                sc = sc + w_sc[h:h + 1, :] * jnp.maximum(d[h], 0.0)
            if masked:
                sc = jnp.where(k0 + k_off <= q_pos, sc, -jnp.inf)
            s_sc[c] = sc
            key = sort_key(sc)
            khi_sc[c] = (key >> 16).astype(I16)
            klo_sc[c] = (key ^ 0x8000).astype(I16)

    def score_pair(c2, carry):
        score_chunks([2 * c2, 2 * c2 + 1], False)
        return carry

    lax.fori_loop(0, nfull // 2, score_pair, 0)

    @pl.when(nfull % 2 == 1)
    def _():
        score_chunks([nfull - 1], False)

    score_chunks([nfull + t for t in range(ndiag)], True)

    npair = (nck + 1) // 2

    @pl.when(nck % 2 == 1)
    def _():
        khi_sc[nck] = jnp.full((ck, tq), I16_MIN, I16)
        klo_sc[nck] = jnp.full((ck, tq), I16_MIN, I16)

    def count16(ref, thr, strict=False):
        def body(c2, parts):
            new = []
            for j, part in enumerate(parts):
                x = ref[2 * c2 + j]
                ones = jnp.where((x > thr) if strict else (x >= thr), jnp.bfloat16(1), jnp.bfloat16(0))
                for r in range(ck // SUB16):
                    part = part + ones[r * SUB16:(r + 1) * SUB16]
                new.append(part)
            return tuple(new)

        zero = jnp.zeros((SUB16, tq), BF16)
        parts = lax.fori_loop(0, npair, body, (zero, zero))
        return jnp.sum(parts[0].astype(F32) + parts[1].astype(F32), axis=0, keepdims=True)

    def bisect16(ref, want):
        def step(it, res):
            cand = res + jnp.left_shift(jnp.int32(1), 15 - it)
            return jnp.where(count16(ref, cand.astype(I16)) >= want, cand, res)

        return lax.fori_loop(0, 16, step, jnp.full((1, tq), I16_MIN, I32))

    hi = bisect16(khi_sc, kf)
    hi16 = hi.astype(I16)
    want_lo = kf - count16(khi_sc, hi16, strict=True)

    def low_half(c, carry):
        klo_sc[c] = jnp.where(khi_sc[c] == hi16, klo_sc[c], jnp.int16(I16_MIN))
        return carry

    lax.fori_loop(0, nck, low_half, 0)
    lo = bisect16(klo_sc, want_lo)
    res = (hi << 16) | ((lo ^ 0x8000) & 0xFFFF)
    seed = pltpu.bitcast(res ^ ((res >> 31) & 0x7FFFFFFF), F32)

    few = (i * tq + lax.broadcasted_iota(I32, (1, tq), 1) + 1).astype(F32) < kf
    floor = jnp.float32(jnp.finfo(jnp.float32).min)

    reducers = {"sum": (jnp.add, jnp.sum), "max": (jnp.maximum, jnp.max), "min": (jnp.minimum, jnp.min)}

    def fold(fn, init, kind):
        pair, red = reducers[kind]

        def body(c, part):
            return pair(part, red(fn(s_sc[c]).reshape(ck // SUB32, SUB32, tq), axis=0))

        part = lax.fori_loop(0, nck, body, jnp.full((SUB32, tq), init, F32))
        return red(part, axis=0, keepdims=True)

    def count(pred):
        return fold(lambda x: jnp.where(pred(x), 1.0, 0.0), 0.0, "sum")

    def any_true(mask):
        return jnp.max(jnp.where(mask, 1.0, 0.0)) > 0.5

    def down_cond(carry):
        g, c_ge = carry
        return any_true(jnp.logical_and(c_ge < kf, jnp.logical_not(few)))

    def down_body(carry):
        g, c_ge = carry
        below = fold(lambda x: jnp.where(x < g, x, -jnp.inf), -jnp.inf, "max")
        g = jnp.where(jnp.logical_and(c_ge < kf, jnp.logical_not(few)), below, g)
        return g, count(lambda x: x >= g)

    g0 = jnp.where(few, floor, seed)
    g, _ = lax.while_loop(down_cond, down_body, (g0, count(lambda x: x >= g0)))

    def up_cond(carry):
        v, c_gt = carry
        return any_true(jnp.logical_and(c_gt >= kf, jnp.logical_not(few)))

    def up_body(carry):
        v, c_gt = carry
        above = fold(lambda x: jnp.where(x > v, x, jnp.inf), jnp.inf, "min")
        v = jnp.where(jnp.logical_and(c_gt >= kf, jnp.logical_not(few)), above, v)
        return v, count(lambda x: x > v)

    v0 = jnp.where(few, floor, fold(lambda x: jnp.where(x >= g, x, jnp.inf), jnp.inf, "min"))
    tau, _ = lax.while_loop(up_cond, up_body, (v0, count(lambda x: x > v0)))
    n_ge = count(lambda x: x >= tau)

    @pl.when(any_true(jnp.logical_and(n_ge > kf, jnp.logical_not(few))))
    def _():
        need = kf - count(lambda x: x > tau)
        incl = (lax.broadcasted_iota(I32, (ck, ck), 1)
                <= lax.broadcasted_iota(I32, (ck, ck), 0)).astype(F32).astype(BF16)

        def body(c, seen):
            x = s_sc[c]
            eq = jnp.where(x == tau, 1.0, 0.0)
            rank = jnp.dot(incl, eq.astype(BF16), preferred_element_type=F32) + seen
            drop = eq * jnp.where(rank > need, 1.0, 0.0)
            s_sc[c] = jnp.where(drop > 0.5, -jnp.inf, x)
            return seen + jnp.sum(eq, axis=0, keepdims=True)

        lax.fori_loop(0, nck, body, jnp.zeros((1, tq), F32))

    m_sc[...] = jnp.full(m_sc.shape, NEG, F32)
    l_sc[...] = jnp.zeros(l_sc.shape, F32)
    acc_sc[...] = jnp.zeros(acc_sc.shape, F32)
    ones_rows = jnp.ones((SUB16, ck), BF16)

    def attend_chunks(cs):
        k0s = [pl.multiple_of(c * ck, ck) for c in cs]
        biases = [jnp.where(s_sc[c] >= tau, 0.0, NEG) for c in cs]
        ss, ps, alphas, vts = {}, {}, {}, {}

        def scores(j, h):
            sl = slice((h // 2) * LANES, (h // 2 + 1) * LANES)
            ss[j, h] = _dot_nt(k_ref[pl.ds(k0s[j], ck), sl], qm_sc[h]) + biases[j]
            if h % 2 == 0:
                vts[j, h // 2] = v_ref[pl.ds(k0s[j], ck), sl].astype(F32).T.astype(BF16)

        def softmax(j, h):
            s = ss.pop((j, h))
            m_prev = m_sc[h:h + 1, :]
            m_next = jnp.maximum(m_prev, jnp.max(s, axis=0, keepdims=True))
            ps[j, h] = jnp.exp(s - m_next).astype(BF16)
            alphas[j, h] = jnp.exp(m_prev - m_next)
            m_sc[h:h + 1, :] = m_next

        def values(j, h):
            e = h % 2
            vt = jnp.concatenate([vts[j, h // 2][e * HEAD_DIM:(e + 1) * HEAD_DIM], ones_rows], axis=0)
            pv = jnp.dot(vt, ps.pop((j, h)), preferred_element_type=F32)
            alpha = alphas.pop((j, h))
            l_sc[h:h + 1, :] = alpha * l_sc[h:h + 1, :] + pv[HEAD_DIM:HEAD_DIM + 1, :]
            acc_sc[h] = alpha * acc_sc[h] + pv[:HEAD_DIM, :]

        work = [(j, h) for j in range(len(cs)) for h in range(N_HEADS)]
        stages = (scores, softmax, values)
        for step in range(len(work) + len(stages) - 1):
            for s, stage in enumerate(stages):
                if 0 <= step - s < len(work):
                    stage(*work[step - s])

    def attend_pair(c2, carry):
        attend_chunks([2 * c2, 2 * c2 + 1])
        return carry

    lax.fori_loop(0, nck // 2, attend_pair, 0)

    @pl.when(nck % 2 == 1)
    def _():
        attend_chunks([nck - 1])

    for p in range(N_PAIRS):
        he, ho = 2 * p, 2 * p + 1
        out_t = jnp.concatenate([acc_sc[he] / l_sc[he:he + 1, :],
                                 acc_sc[ho] / l_sc[ho:ho + 1, :]], axis=0)
        _gated_store(o_ref, g_ref, p, out_t.T)


def _dsa_call(zp, zr, batch, seq, tq, ck):
    topk = min(TOPK_MAX, seq // 4)
    nq = seq // tq
    assert seq // SUB16 <= 256
    n_even = 2 * pl.cdiv(seq // ck, 2)
    kern = functools.partial(_dsa_kernel, tq=tq, ck=ck, topk=topk)
    return pl.pallas_call(
        kern,
        out_shape=jax.ShapeDtypeStruct((batch * seq, MIX_W), BF16),
        grid=(batch, nq),
        in_specs=[
            pl.BlockSpec((tq, MIX_W), lambda b, i: (b * nq + i, (C_AQ - NP_PLAIN) // MIX_W)),
            pl.BlockSpec((seq, MIX_W), lambda b, i: (b, (C_AK - NP_PLAIN) // MIX_W)),
            pl.BlockSpec((seq, MIX_W), lambda b, i: (b, C_AV // MIX_W)),
            pl.BlockSpec((tq, MIX_W), lambda b, i: (b * nq + i, C_AG // MIX_W)),
            pl.BlockSpec((tq, 2 * LANES), lambda b, i: (b * nq + i, (C_IQ - NP_PLAIN) // (2 * LANES))),
            pl.BlockSpec((seq, LANES), lambda b, i: (b, (C_IK2 - NP_PLAIN) // LANES)),
            pl.BlockSpec((tq, LANES), lambda b, i: (b * nq + i, (C_IW - NP_PLAIN) // LANES)),
        ],
        out_specs=pl.BlockSpec((tq, MIX_W), lambda b, i: (b * nq + i, 0)),
        scratch_shapes=[
            pltpu.VMEM((seq // ck, ck, tq), F32),
            pltpu.VMEM((n_even, ck, tq), I16),
            pltpu.VMEM((n_even, ck, tq), I16),
            pltpu.VMEM((N_HEADS, tq, LANES), BF16),
            pltpu.VMEM((IDX_HEADS, tq, LANES), BF16),
            pltpu.VMEM((SUB32, tq), F32),
            pltpu.VMEM((N_HEADS, tq), F32),
            pltpu.VMEM((N_HEADS, tq), F32),
            pltpu.VMEM((N_HEADS, HEAD_DIM, tq), F32),
        ],
        compiler_params=_cparams(2),
        name="dsa_attention",
    )(zr, zr, zp, zp, zr, zr, zr)


def _swa_kernel(sinks_ref, q_ref, k2_ref, k2s_ref, v2_ref, v2s_ref, g_ref, o_ref, *, tq, layer):
    i = pl.program_id(1)
    span = tq + WINDOW
    kstart = pl.multiple_of(jnp.maximum(i * tq - WINDOW, 0), WINDOW)
    lo_half = _lo_half(tq)
    row = i * tq + lax.broadcasted_iota(I32, (tq, span), 0)
    col = kstart + lax.broadcasted_iota(I32, (tq, span), 1)
    bias = jnp.where((col <= row) & (col > row - WINDOW), 0.0, NEG)
    kv = {False: (k2s_ref[pl.ds(kstart, span), :], v2s_ref[pl.ds(kstart, span), :]),
          True: (k2_ref[pl.ds(kstart, span), :], v2_ref[pl.ds(kstart, span), :])}
    grp = N_HEADS // B_KV_HEADS
    ss, ps, dens, outs = {}, {}, {}, {}

    def scores(h):
        p, e = divmod(h, 2)
        qp = q_ref[:, p * LANES:(p + 1) * LANES].astype(F32) * (HEAD_DIM ** -0.5)
        qm = jnp.where(lo_half if e == 0 else ~lo_half, qp, 0.0).astype(BF16)
        ss[h] = _dot_nt(qm, kv[(h // grp) == e][0]) + bias

    def softmax(h):
        s = ss.pop(h)
        sink = sinks_ref[layer, h]
        m = jnp.maximum(jnp.max(s, axis=1, keepdims=True), sink)
        pe = jnp.exp(s - m)
        dens[h] = jnp.sum(pe, axis=1, keepdims=True) + jnp.exp(sink - m)
        ps[h] = pe.astype(BF16)

    def values(h):
        vv = kv[(h // grp) == (h % 2)][1]
        outs[h] = jnp.dot(ps.pop(h), vv, preferred_element_type=F32) / dens.pop(h)
        if h % 2 == 1:
            _gated_store(o_ref, g_ref, h // 2, jnp.where(lo_half, outs.pop(h - 1), outs.pop(h)))

    stages = (scores, softmax, values)
    for step in range(N_HEADS + len(stages) - 1):
        for s_i, stage in enumerate(stages):
            if 0 <= step - s_i < N_HEADS:
                stage(step - s_i)


def _swa_call(zp, zr, sinks, layer, batch, seq, tq):
    nq = seq // tq
    kv_spec = lambda c: pl.BlockSpec((seq, LANES), lambda b, i: (b, (c - NP_PLAIN) // LANES))
    return pl.pallas_call(
        functools.partial(_swa_kernel, tq=tq, layer=layer),
        out_shape=jax.ShapeDtypeStruct((batch * seq, MIX_W), BF16),
        grid=(batch, nq),
        in_specs=[
            pl.BlockSpec(memory_space=pltpu.SMEM),
            pl.BlockSpec((tq, MIX_W), lambda b, i: (b * nq + i, (C_BQ - NP_PLAIN) // MIX_W)),
            kv_spec(C_BK2), kv_spec(C_BK2S), kv_spec(C_BV2), kv_spec(C_BV2S),
            pl.BlockSpec((tq, MIX_W), lambda b, i: (b * nq + i, C_BG // MIX_W)),
        ],
        out_specs=pl.BlockSpec((tq, MIX_W), lambda b, i: (b * nq + i, 0)),
        compiler_params=_cparams(2),
        name="swa_sinks_attention",
    )(sinks, zr, zr, zr, zr, zr, zp)


def _sb_kernel(q_ref, k_ref, v_ref, g_ref, o_ref, qm_sc, run_sc, acc_sc, *, tq, ck):
    i = pl.program_id(1)
    ndiag = tq // ck
    nck = (i + 1) * ndiag
    rep = ck // LANES
    lo_half = _lo_half(tq)
    row = i * tq + lax.broadcasted_iota(I32, (tq, ck), 0)
    col_l = lax.broadcasted_iota(I32, (tq, ck), 1)
    j_idx = lax.broadcasted_iota(I32, (2 * ck, ck), 0)
    j_idx = jnp.where(j_idx >= ck, j_idx - ck, j_idx)
    neg_later2 = jnp.where(j_idx > lax.broadcasted_iota(I32, (2 * ck, ck), 1),
                           -1.0, 0.0).astype(BF16)

    for p in range(N_PAIRS):
        qp = q_ref[:, p * LANES:(p + 1) * LANES].astype(F32) * (HEAD_DIM ** -0.5)
        qm_sc[2 * p] = jnp.where(lo_half, qp, 0.0).astype(BF16)
        qm_sc[2 * p + 1] = jnp.where(lo_half, 0.0, qp).astype(BF16)
    run_sc[...] = jnp.zeros(run_sc.shape, F32)
    acc_sc[...] = jnp.zeros(acc_sc.shape, F32)

    def chunk(t, masked):
        k0 = pl.multiple_of((nck - 1 - t) * ck, ck)
        strict = (k0 + col_l < row) if masked else None
        zs, nlms, lbs, afters = {}, {}, {}, {}

        def scores(h):
            kp = k_ref[pl.ds(k0, ck), (h // 2) * LANES:(h // 2 + 1) * LANES]
            zs[h] = _dot_nt(qm_sc[h], kp) * LOG2E

        def logs(h):
            z2 = zs.pop(h)
            nlm = jnp.maximum(z2, 0.0) + jnp.log2(1.0 + jnp.exp2(-jnp.abs(z2)))
            lbs[h] = z2 - nlm
            nlms[h] = jnp.where(strict, nlm, 0.0) if masked else nlm

        def cumsum(h):
            hi = nlms[h].astype(BF16)
            lo = (nlms[h] - hi.astype(F32)).astype(BF16)
            afters[h] = jnp.dot(jnp.concatenate([hi, lo], axis=1), neg_later2,
                                preferred_element_type=F32)

        def weights(h):
            run = run_sc[h]
            a = jnp.exp2(lbs.pop(h) + afters.pop(h) + jnp.tile(run, (1, rep)))
            if masked:
                a = jnp.where(strict, a, 0.0)
            vp = v_ref[pl.ds(k0, ck), (h // 2) * LANES:(h // 2 + 1) * LANES]
            acc_sc[h] += jnp.dot(a.astype(BF16), vp, preferred_element_type=F32)
            run_sc[h] = run - jnp.sum(nlms.pop(h), axis=1, keepdims=True)

        stages = (scores, logs, cumsum, weights)
        for step in range(N_HEADS + len(stages) - 1):
            for s, stage in enumerate(stages):
                if 0 <= step - s < N_HEADS:
                    stage(step - s)

    for t in range(ndiag):
        chunk(t, True)

    def live():
        return jnp.max(run_sc[...]) > RUN_FLOOR

    def body(carry):
        t, _ = carry
        chunk(t, False)
        return t + 1, live()

    lax.while_loop(lambda carry: jnp.logical_and(carry[0] < nck, carry[1]), body,
                   (jnp.int32(ndiag), live()))

    for p in range(N_PAIRS):
        _gated_store(o_ref, g_ref, p, jnp.where(lo_half, acc_sc[2 * p], acc_sc[2 * p + 1]))


def _sb_call(z, batch, seq, tq, ck):
    nq = seq // tq
    state = pltpu.VMEM((N_HEADS, tq, LANES), F32)
    return pl.pallas_call(
        functools.partial(_sb_kernel, tq=tq, ck=ck),
        out_shape=jax.ShapeDtypeStruct((batch * seq, MIX_W), BF16),
        grid=(batch, nq),
        in_specs=[
            pl.BlockSpec((tq, MIX_W), lambda b, i: (b * nq + i, C_CQ // MIX_W)),
            pl.BlockSpec((seq, MIX_W), lambda b, i: (b, C_CK // MIX_W)),
            pl.BlockSpec((seq, MIX_W), lambda b, i: (b, C_CV // MIX_W)),
            pl.BlockSpec((tq, MIX_W), lambda b, i: (b * nq + i, C_CG // MIX_W)),
        ],
        out_specs=pl.BlockSpec((tq, MIX_W), lambda b, i: (b * nq + i, 0)),
        scratch_shapes=[pltpu.VMEM((N_HEADS, tq, LANES), BF16), state, state],
        compiler_params=_cparams(2),
        name="stick_breaking_attention",
    )(z, z, z, z)


def _merge_kernel(ya_ref, yb_ref, yc_ref, ma_ref, mb_ref, mc_ref, x_ref, gate_ref,
                  wa_ref, wb_ref, wc_ref, wo_ref, fg_ref, o_ref, *, final):
    def branch(y_ref, m_ref, w_ref):
        pr = jnp.dot(y_ref[...], w_ref[0], preferred_element_type=F32)
        return _sigmoid(m_ref[...].astype(F32)) * pr

    merged = (branch(ya_ref, ma_ref, wa_ref) + branch(yb_ref, mb_ref, wb_ref)
              + branch(yc_ref, mc_ref, wc_ref))
    out = x_ref[...] + gate_ref[0] * jnp.dot(merged.astype(BF16), wo_ref[0],
                                             preferred_element_type=F32)
    if final:
        out = out * lax.rsqrt(jnp.mean(out * out, axis=-1, keepdims=True) + EPS) * fg_ref[...]
    o_ref[...] = out


def _merge_call(ya, yb, yc, z, h, mod3, wa, wb, wc, wo, fg, l, seq, tm, final):
    rows, d = h.shape
    nqb = seq // tm
    y_spec = pl.BlockSpec((tm, MIX_W), lambda i: (i, 0))
    m_spec = lambda c: pl.BlockSpec((tm, d), lambda i: (i, c // d))
    w_spec = pl.BlockSpec((1, MIX_W, d), lambda i: (l, 0, 0))
    return pl.pallas_call(
        functools.partial(_merge_kernel, final=final),
        out_shape=jax.ShapeDtypeStruct((rows, d), F32),
        grid=(rows // tm,),
        in_specs=[
            y_spec, y_spec, y_spec, m_spec(C_MA), m_spec(C_MB), m_spec(C_MC),
            pl.BlockSpec((tm, d), lambda i: (i, 0)),
            pl.BlockSpec((1, 1, d), lambda i: (l * MOD_ROWS + i // nqb, 0, 2)),
            w_spec, w_spec, w_spec,
            pl.BlockSpec((1, d, d), lambda i: (l, 0, 0)),
            pl.BlockSpec((1, d), lambda i: (0, 0)),
        ],
        out_specs=pl.BlockSpec((tm, d), lambda i: (i, 0)),
        compiler_params=_cparams(1),
        name="merge_outproj_residual",
    )(ya, yb, yc, z, z, z, h, mod3, wa, wb, wc, wo, fg.reshape(1, d))


def _tiles(seq):
    big = 512 if seq % 512 == 0 else 256
    return min(1024, seq), big, 256, big


def kernel(x, c, norm_g, w_ada, b_ada, w_in, sinks, w_br_a, w_br_b, w_br_c, w_out, final_g):
    batch, seq, d = x.shape
    depth = w_in.shape[0]
    assert d == D_MODEL and seq % 256 == 0 and seq >= 256 + WINDOW and batch <= MOD_ROWS
    tm_in, tq_a, tq, tm_mg = _tiles(seq)

    c_pad = jnp.zeros((MOD_ROWS, d), F32).at[:batch].set(c)
    mod3 = _mod_call(c_pad, w_ada, b_ada).reshape(depth * MOD_ROWS, 1, 3 * d)
    tabs = _rope_tables(seq)
    w_all = _prep_call(w_in)
    norm_g3 = norm_g.reshape(depth, 1, d)
    wa, wb, wc, wo = (w.astype(BF16) for w in (w_br_a, w_br_b, w_br_c, w_out))
    h = x.reshape(batch * seq, d)
    for l in range(depth):
        zp = _inproj_call(h, mod3, norm_g3, w_all, tabs, l, seq, tm_in, rope=False)
        zr = _inproj_call(h, mod3, norm_g3, w_all, tabs, l, seq, tm_in, rope=True)
        ya = _dsa_call(zp, zr, batch, seq, tq_a, tq_a)
        yb = _swa_call(zp, zr, sinks, l, batch, seq, tq)
        yc = _sb_call(zp, batch, seq, tq, tq)
        h = _merge_call(ya, yb, yc, zp, h, mod3, wa, wb, wc, wo, final_g, l, seq, tm_mg,
                        final=(l == depth - 1))
    return h.reshape(batch, seq, d)
```

```python
import functools

import jax
import jax.numpy as jnp
import numpy as np
from jax import lax
from jax.experimental import pallas as pl
from jax.experimental.pallas import tpu as pltpu

F32 = jnp.float32
BF16 = jnp.bfloat16
I32 = jnp.int32
I16 = jnp.int16

D_MODEL = 1024
HEAD_DIM = 64
ROPE_THETA = 10000.0
EPS = 1e-6
N_HEADS = 8
N_PAIRS = N_HEADS // 2
IDX_HEADS = 4
TOPK_MAX = 256
B_KV_HEADS = 2
WINDOW = 128
MIX_W = N_HEADS * HEAD_DIM

LANES = 128
MOD_ROWS = 8
VMEM_LIMIT = 56 * 1024 * 1024
NEG = -1e30
LOG2E = 1.4426950408889634
RUN_FLOOR = -160.0
INT_MIN = -(2 ** 31)
I16_MIN = -(2 ** 15)
SUB32, SUB16 = 8, 16

C_MA, C_MB, C_MC = 0, 1024, 2048
C_AV, C_AG, C_BG, C_CQ, C_CK, C_CV, C_CG = 3072, 3584, 4096, 4608, 5120, 5632, 6144
C_AQ, C_AK, C_BQ = 6656, 7168, 7680
C_IQ = 8192
C_IK2, C_BK2, C_BK2S, C_BV2, C_BV2S, C_IW = 8448, 8576, 8704, 8832, 8960, 9088
NP = 9216
TN = 512
TN_PLAIN = 1664
INPROJ_SPLIT = 4
NP_PLAIN = C_AQ
NP_ROPE = NP - NP_PLAIN
assert NP_PLAIN % TN_PLAIN == 0 and NP_PLAIN % TN == 0 and NP_ROPE % TN == 0 and C_BK2S == NP - TN

_SPLIT_NAMES = ("aq", "ak", "av", "ag", "iq", "ik", "iw", "bq", "bk", "bv", "bg",
                "cq", "ck", "cv", "cg", "ma", "mb", "mc")
_SPLIT_SIZES = (512, 512, 512, 512, 256, 64, 4, 512, 128, 128, 512,
                512, 512, 512, 512, 1024, 1024, 1024)


def _cparams(n_grid):
    return pltpu.CompilerParams(dimension_semantics=("arbitrary",) * n_grid,
                                vmem_limit_bytes=VMEM_LIMIT)


def _sigmoid(x):
    return 1.0 / (1.0 + jnp.exp(-x))


def _dot_nt(a, b):
    return lax.dot_general(a, b, (((1,), (1,)), ((), ())), preferred_element_type=F32)


def _lo_half(rows):
    return lax.broadcasted_iota(I32, (rows, LANES), 1) < HEAD_DIM


def _mod_kernel(c_ref, w_ref, b_ref, o_ref):
    c = c_ref[...]
    cs = c * _sigmoid(c)
    o_ref[0] = jnp.dot(cs, w_ref[0], preferred_element_type=F32,
                       precision=lax.Precision.HIGHEST) + b_ref[0]


def _mod_call(c_pad, w_ada, b_ada):
    depth, d, d3 = w_ada.shape
    rows = c_pad.shape[0]
    return pl.pallas_call(
        _mod_kernel,
        out_shape=jax.ShapeDtypeStruct((depth, rows, d3), F32),
        grid=(depth, d3 // d),
        in_specs=[
            pl.BlockSpec((rows, d), lambda l, j: (0, 0)),
            pl.BlockSpec((1, d, d), lambda l, j: (l, 0, j)),
            pl.BlockSpec((1, 1, d), lambda l, j: (l, 0, j)),
        ],
        out_specs=pl.BlockSpec((1, rows, d), lambda l, j: (l, 0, j)),
        compiler_params=_cparams(2),
        name="adaln_mod",
    )(c_pad, w_ada, b_ada.reshape(depth, 1, d3))


def _inproj_kernel(x_ref, shift_ref, scale_ref, g_ref, w_ref, *rest, rope):
    tab_ref = rest[0] if rope else None
    o_ref, u_sc = rest[-2:]
    j = pl.program_id(1)

    @pl.when(j == 0)
    def _():
        x = x_ref[...]
        y = x * lax.rsqrt(jnp.mean(x * x, axis=-1, keepdims=True) + EPS) * g_ref[0]
        u_sc[...] = (y * (1.0 + scale_ref[0]) + shift_ref[0]).astype(BF16)

    tm, tn = o_ref.shape
    sub_rows = tm // INPROJ_SPLIT
    if rope:
        lane = lax.broadcasted_iota(I32, (sub_rows, tn), 1)
        first = (lane & (HEAD_DIM - 1)) < HEAD_DIM // 2
        n_rope = jnp.where(j == pl.num_programs(1) - 1, LANES, tn)
    for r in range(INPROJ_SPLIT):
        rows = slice(r * sub_rows, (r + 1) * sub_rows)
        acc = jnp.dot(u_sc[rows, :], w_ref[0], preferred_element_type=F32)
        if rope:
            partner = jnp.where(first, pltpu.roll(acc, tn - HEAD_DIM // 2, 1),
                                pltpu.roll(acc, HEAD_DIM // 2, 1))
            roped = (acc * jnp.tile(tab_ref[0, rows, :], (1, tn // LANES))
                     + partner * jnp.tile(tab_ref[1, rows, :], (1, tn // LANES)))
            acc = jnp.where(lane < n_rope, roped, acc)
        o_ref[rows, :] = acc.astype(BF16)


def _inproj_call(h, mod3, norm_g, w, tabs, l, seq, tm, rope):
    rows, d = h.shape
    nqb = seq // tm
    tn = TN if rope else TN_PLAIN
    col0, ncol = (NP_PLAIN // TN, NP_ROPE) if rope else (0, NP_PLAIN)
    in_specs = [
        pl.BlockSpec((tm, d), lambda i, j: (i, 0)),
        pl.BlockSpec((1, 1, d), lambda i, j: (l * MOD_ROWS + i // nqb, 0, 0)),
        pl.BlockSpec((1, 1, d), lambda i, j: (l * MOD_ROWS + i // nqb, 0, 1)),
        pl.BlockSpec((1, 1, d), lambda i, j: (l, 0, 0)),
        pl.BlockSpec((1, d, tn), lambda i, j: (l, 0, col0 + j)),
    ]
    args = [h, mod3, mod3, norm_g, w]
    if rope:
        in_specs.append(pl.BlockSpec((2, tm, LANES), lambda i, j: (0, i % nqb, 0)))
        args.append(tabs)
    return pl.pallas_call(
        functools.partial(_inproj_kernel, rope=rope),
        out_shape=jax.ShapeDtypeStruct((rows, ncol), BF16),
        grid=(rows // tm, ncol // tn),
        in_specs=in_specs,
        out_specs=pl.BlockSpec((tm, tn), lambda i, j: (i, j)),
        scratch_shapes=[pltpu.VMEM((tm, d), BF16)],
        compiler_params=_cparams(2),
        name="norm_inproj_rope" if rope else "norm_inproj",
    )(*args)


def _rope_tables(seq):
    half = HEAD_DIM // 2
    inv = 1.0 / (ROPE_THETA ** (jnp.arange(0, HEAD_DIM, 2, dtype=F32) / HEAD_DIM))
    ang = jnp.arange(seq, dtype=F32)[:, None] * inv[None, :]
    lane = np.arange(LANES)
    sign = np.where((lane % HEAD_DIM) < half, -1.0, 1.0).astype(np.float32)
    return jnp.stack([jnp.cos(ang)[:, lane % half], jnp.sin(ang)[:, lane % half] * sign[None, :]])


_SRC = dict(zip(_SPLIT_NAMES, np.cumsum((0,) + _SPLIT_SIZES[:-1]).tolist()))
_WHOLE = ((C_MA, "ma"), (C_MB, "mb"), (C_MC, "mc"), (C_AV, "av"), (C_AG, "ag"), (C_BG, "bg"),
          (C_CQ, "cq"), (C_CK, "ck"), (C_CV, "cv"), (C_CG, "cg"), (C_AQ, "aq"), (C_AK, "ak"),
          (C_BQ, "bq"), (C_IQ, "iq"), (C_BK2, "bk"), (C_BV2, "bv"))
_HALVES = ((C_IK2, _SRC["ik"], _SRC["ik"]),
           (C_BK2S, _SRC["bk"] + HEAD_DIM, _SRC["bk"]),
           (C_BV2S, _SRC["bv"] + HEAD_DIM, _SRC["bv"]))


def _prep_kernel(w_ref, o_ref):
    rows = w_ref.shape[1]
    lane = lax.broadcasted_iota(I32, (rows, LANES), 1)
    sizes = dict(zip(_SPLIT_NAMES, _SPLIT_SIZES))
    for dst, name in _WHOLE:
        o_ref[0, :, dst:dst + sizes[name]] = w_ref[0, :, _SRC[name]:_SRC[name] + sizes[name]].astype(BF16)
    for dst, lo, hi in _HALVES:
        low = w_ref[0, :, lo:lo + LANES]
        high = w_ref[0, :, hi - HEAD_DIM:hi - HEAD_DIM + LANES]
        o_ref[0, :, dst:dst + LANES] = jnp.where(lane < HEAD_DIM, low, high).astype(BF16)
    iw = w_ref[0, :, _SRC["iw"]:_SRC["iw"] + LANES]
    o_ref[0, :, C_IW:C_IW + LANES] = jnp.where(lane < sizes["iw"], iw, 0.0).astype(BF16)


def _prep_call(w):
    depth, d, n = w.shape
    rows = 128
    return pl.pallas_call(
        _prep_kernel,
        out_shape=jax.ShapeDtypeStruct((depth, d, NP), BF16),
        grid=(depth, d // rows),
        in_specs=[pl.BlockSpec((1, rows, n), lambda l, i: (l, i, 0))],
        out_specs=pl.BlockSpec((1, rows, NP), lambda l, i: (l, i, 0)),
        compiler_params=_cparams(2),
        name="inproj_weight_layout",
    )(w)


def _gated_store(o_ref, g_ref, p, val):
    sl = slice(p * LANES, (p + 1) * LANES)
    g = g_ref[:, sl].astype(F32)
    o_ref[:, sl] = (val * (g * _sigmoid(g))).astype(o_ref.dtype)


def _dsa_kernel(q_ref, k_ref, v_ref, g_ref, iq_ref, ik_ref, iw_ref, o_ref,
                s_sc, khi_sc, klo_sc, qm_sc, iqm_sc, w_sc, m_sc, l_sc, acc_sc, *, tq, ck, topk):
    i = pl.program_id(1)
    ndiag = tq // ck
    nck = (i + 1) * ndiag
    nfull = nck - ndiag
    lo_half = _lo_half(tq)
    kf = jnp.float32(topk)

    for p in range(N_PAIRS):
        qp = q_ref[:, p * LANES:(p + 1) * LANES].astype(F32) * (HEAD_DIM ** -0.5)
        qm_sc[2 * p] = jnp.where(lo_half, qp, 0.0).astype(BF16)
        qm_sc[2 * p + 1] = jnp.where(lo_half, 0.0, qp).astype(BF16)
    for p in range(IDX_HEADS // 2):
        t = iq_ref[:, p * LANES:(p + 1) * LANES].astype(F32)
        iqm_sc[2 * p] = jnp.where(lo_half, t, 0.0).astype(BF16)
        iqm_sc[2 * p + 1] = jnp.where(lo_half, 0.0, t).astype(BF16)
    w_sc[...] = iw_ref[...].astype(F32).T[:SUB32] * ((IDX_HEADS ** -0.5) * (HEAD_DIM ** -0.5))

    q_pos = i * tq + lax.broadcasted_iota(I32, (ck, tq), 1)
    k_off = lax.broadcasted_iota(I32, (ck, tq), 0)

    def sort_key(x):
        bits = pltpu.bitcast(x, I32)
        return bits ^ ((bits >> 31) & 0x7FFFFFFF)

    def score_chunks(cs, masked):
        k0s = [pl.multiple_of(c * ck, ck) for c in cs]
        ds = [[_dot_nt(ik_ref[pl.ds(k0, ck), :], iqm_sc[h]) for h in range(IDX_HEADS)] for k0 in k0s]
        for c, k0, d in zip(cs, k0s, ds):
            sc = w_sc[0:1, :] * jnp.maximum(d[0], 0.0)
            for h in range(1, IDX_HEADS):
                sc = sc + w_sc[h:h + 1, :] * jnp.maximum(d[h], 0.0)
            if masked:
                sc = jnp.where(k0 + k_off <= q_pos, sc, -jnp.inf)
            s_sc[c] = sc
            key = sort_key(sc)
            khi_sc[c] = (key >> 16).astype(I16)
            klo_sc[c] = (key ^ 0x8000).astype(I16)

    def score_pair(c2, carry):
        score_chunks([2 * c2, 2 * c2 + 1], False)
        return carry

    lax.fori_loop(0, nfull // 2, score_pair, 0)

    @pl.when(nfull % 2 == 1)
    def _():
        score_chunks([nfull - 1], False)

    score_chunks([nfull + t for t in range(ndiag)], True)

    npair = (nck + 1) // 2

    @pl.when(nck % 2 == 1)
    def _():
        khi_sc[nck] = jnp.full((ck, tq), I16_MIN, I16)
        klo_sc[nck] = jnp.full((ck, tq), I16_MIN, I16)

    def count16(ref, thr, strict=False):
        def body(c2, parts):
            new = []
            for j, part in enumerate(parts):
                x = ref[2 * c2 + j]
                ones = jnp.where((x > thr) if strict else (x >= thr), jnp.bfloat16(1), jnp.bfloat16(0))
                for r in range(ck // SUB16):
                    part = part + ones[r * SUB16:(r + 1) * SUB16]
                new.append(part)
            return tuple(new)

        zero = jnp.zeros((SUB16, tq), BF16)
        parts = lax.fori_loop(0, npair, body, (zero, zero))
        return jnp.sum(parts[0].astype(F32) + parts[1].astype(F32), axis=0, keepdims=True)

    def bisect16(ref, want):
        def step(it, res):
            cand = res + jnp.left_shift(jnp.int32(1), 15 - it)
            return jnp.where(count16(ref, cand.astype(I16)) >= want, cand, res)

        return lax.fori_loop(0, 16, step, jnp.full((1, tq), I16_MIN, I32))

    hi = bisect16(khi_sc, kf)
    hi16 = hi.astype(I16)
    want_lo = kf - count16(khi_sc, hi16, strict=True)

    def low_half(c, carry):
        klo_sc[c] = jnp.where(khi_sc[c] == hi16, klo_sc[c], jnp.int16(I16_MIN))
        return carry

    lax.fori_loop(0, nck, low_half, 0)
    lo = bisect16(klo_sc, want_lo)
    res = (hi << 16) | ((lo ^ 0x8000) & 0xFFFF)
    seed = pltpu.bitcast(res ^ ((res >> 31) & 0x7FFFFFFF), F32)

    few = (i * tq + lax.broadcasted_iota(I32, (1, tq), 1) + 1).astype(F32) < kf
    floor = jnp.float32(jnp.finfo(jnp.float32).min)

    reducers = {"sum": (jnp.add, jnp.sum), "max": (jnp.maximum, jnp.max), "min": (jnp.minimum, jnp.min)}

    def fold(fn, init, kind):
        pair, red = reducers[kind]

        def body(c, part):
            return pair(part, red(fn(s_sc[c]).reshape(ck // SUB32, SUB32, tq), axis=0))

        part = lax.fori_loop(0, nck, body, jnp.full((SUB32, tq), init, F32))
        return red(part, axis=0, keepdims=True)

    def count(pred):
        return fold(lambda x: jnp.where(pred(x), 1.0, 0.0), 0.0, "sum")

    def any_true(mask):
        return jnp.max(jnp.where(mask, 1.0, 0.0)) > 0.5

    def down_cond(carry):
        g, c_ge = carry
        return any_true(jnp.logical_and(c_ge < kf, jnp.logical_not(few)))

    def down_body(carry):
        g, c_ge = carry
        below = fold(lambda x: jnp.where(x < g, x, -jnp.inf), -jnp.inf, "max")
        g = jnp.where(jnp.logical_and(c_ge < kf, jnp.logical_not(few)), below, g)
        return g, count(lambda x: x >= g)

    g0 = jnp.where(few, floor, seed)
    g, _ = lax.while_loop(down_cond, down_body, (g0, count(lambda x: x >= g0)))

    def up_cond(carry):
        v, c_gt = carry
        return any_true(jnp.logical_and(c_gt >= kf, jnp.logical_not(few)))

    def up_body(carry):
        v, c_gt = carry
        above = fold(lambda x: jnp.where(x > v, x, jnp.inf), jnp.inf, "min")
        v = jnp.where(jnp.logical_and(c_gt >= kf, jnp.logical_not(few)), above, v)
        return v, count(lambda x: x > v)

    v0 = jnp.where(few, floor, fold(lambda x: jnp.where(x >= g, x, jnp.inf), jnp.inf, "min"))
    tau, _ = lax.while_loop(up_cond, up_body, (v0, count(lambda x: x > v0)))
    n_ge = count(lambda x: x >= tau)

    @pl.when(any_true(jnp.logical_and(n_ge > kf, jnp.logical_not(few))))
    def _():
        need = kf - count(lambda x: x > tau)
        incl = (lax.broadcasted_iota(I32, (ck, ck), 1)
                <= lax.broadcasted_iota(I32, (ck, ck), 0)).astype(F32).astype(BF16)

        def body(c, seen):
            x = s_sc[c]
            eq = jnp.where(x == tau, 1.0, 0.0)
            rank = jnp.dot(incl, eq.astype(BF16), preferred_element_type=F32) + seen
            drop = eq * jnp.where(rank > need, 1.0, 0.0)
            s_sc[c] = jnp.where(drop > 0.5, -jnp.inf, x)
            return seen + jnp.sum(eq, axis=0, keepdims=True)

        lax.fori_loop(0, nck, body, jnp.zeros((1, tq), F32))

    m_sc[...] = jnp.full(m_sc.shape, NEG, F32)
    l_sc[...] = jnp.zeros(l_sc.shape, F32)
    acc_sc[...] = jnp.zeros(acc_sc.shape, F32)
    ones_rows = jnp.ones((SUB16, ck), BF16)

    def attend_chunks(cs):
        k0s = [pl.multiple_of(c * ck, ck) for c in cs]
        biases = [jnp.where(s_sc[c] >= tau, 0.0, NEG) for c in cs]
        ss, ps, alphas, vts = {}, {}, {}, {}

        def scores(j, h):
            sl = slice((h // 2) * LANES, (h // 2 + 1) * LANES)
            ss[j, h] = _dot_nt(k_ref[pl.ds(k0s[j], ck), sl], qm_sc[h]) + biases[j]
            if h % 2 == 0:
                vts[j, h // 2] = v_ref[pl.ds(k0s[j], ck), sl].astype(F32).T.astype(BF16)

        def softmax(j, h):
            s = ss.pop((j, h))
            m_prev = m_sc[h:h + 1, :]
            m_next = jnp.maximum(m_prev, jnp.max(s, axis=0, keepdims=True))
            ps[j, h] = jnp.exp(s - m_next).astype(BF16)
            alphas[j, h] = jnp.exp(m_prev - m_next)
            m_sc[h:h + 1, :] = m_next

        def values(j, h):
            e = h % 2
            vt = jnp.concatenate([vts[j, h // 2][e * HEAD_DIM:(e + 1) * HEAD_DIM], ones_rows], axis=0)
            pv = jnp.dot(vt, ps.pop((j, h)), preferred_element_type=F32)
            alpha = alphas.pop((j, h))
            l_sc[h:h + 1, :] = alpha * l_sc[h:h + 1, :] + pv[HEAD_DIM:HEAD_DIM + 1, :]
            acc_sc[h] = alpha * acc_sc[h] + pv[:HEAD_DIM, :]

        work = [(j, h) for j in range(len(cs)) for h in range(N_HEADS)]
        stages = (scores, softmax, values)
        for step in range(len(work) + len(stages) - 1):
            for s, stage in enumerate(stages):
                if 0 <= step - s < len(work):
                    stage(*work[step - s])

    def attend_pair(c2, carry):
        attend_chunks([2 * c2, 2 * c2 + 1])
        return carry

    lax.fori_loop(0, nck // 2, attend_pair, 0)

    @pl.when(nck % 2 == 1)
    def _():
        attend_chunks([nck - 1])

    for p in range(N_PAIRS):
        he, ho = 2 * p, 2 * p + 1
        out_t = jnp.concatenate([acc_sc[he] / l_sc[he:he + 1, :],
                                 acc_sc[ho] / l_sc[ho:ho + 1, :]], axis=0)
        _gated_store(o_ref, g_ref, p, out_t.T)


def _dsa_call(zp, zr, batch, seq, tq, ck):
    topk = min(TOPK_MAX, seq // 4)
    nq = seq // tq
    assert seq // SUB16 <= 256
    n_even = 2 * pl.cdiv(seq // ck, 2)
    kern = functools.partial(_dsa_kernel, tq=tq, ck=ck, topk=topk)
    return pl.pallas_call(
        kern,
        out_shape=jax.ShapeDtypeStruct((batch * seq, MIX_W), BF16),
        grid=(batch, nq),
        in_specs=[
            pl.BlockSpec((tq, MIX_W), lambda b, i: (b * nq + i, (C_AQ - NP_PLAIN) // MIX_W)),
            pl.BlockSpec((seq, MIX_W), lambda b, i: (b, (C_AK - NP_PLAIN) // MIX_W)),
            pl.BlockSpec((seq, MIX_W), lambda b, i: (b, C_AV // MIX_W)),
            pl.BlockSpec((tq, MIX_W), lambda b, i: (b * nq + i, C_AG // MIX_W)),
            pl.BlockSpec((tq, 2 * LANES), lambda b, i: (b * nq + i, (C_IQ - NP_PLAIN) // (2 * LANES))),
            pl.BlockSpec((seq, LANES), lambda b, i: (b, (C_IK2 - NP_PLAIN) // LANES)),
            pl.BlockSpec((tq, LANES), lambda b, i: (b * nq + i, (C_IW - NP_PLAIN) // LANES)),
        ],
        out_specs=pl.BlockSpec((tq, MIX_W), lambda b, i: (b * nq + i, 0)),
        scratch_shapes=[
            pltpu.VMEM((seq // ck, ck, tq), F32),
            pltpu.VMEM((n_even, ck, tq), I16),
            pltpu.VMEM((n_even, ck, tq), I16),
            pltpu.VMEM((N_HEADS, tq, LANES), BF16),
            pltpu.VMEM((IDX_HEADS, tq, LANES), BF16),
            pltpu.VMEM((SUB32, tq), F32),
            pltpu.VMEM((N_HEADS, tq), F32),
            pltpu.VMEM((N_HEADS, tq), F32),
            pltpu.VMEM((N_HEADS, HEAD_DIM, tq), F32),
        ],
        compiler_params=_cparams(2),
        name="dsa_attention",
    )(zr, zr, zp, zp, zr, zr, zr)


def _swa_kernel(sinks_ref, q_ref, k2_ref, k2s_ref, v2_ref, v2s_ref, g_ref, o_ref, *, tq, layer):
    i = pl.program_id(1)
    span = tq + WINDOW
    kstart = pl.multiple_of(jnp.maximum(i * tq - WINDOW, 0), WINDOW)
    lo_half = _lo_half(tq)
    row = i * tq + lax.broadcasted_iota(I32, (tq, span), 0)
    col = kstart + lax.broadcasted_iota(I32, (tq, span), 1)
    bias = jnp.where((col <= row) & (col > row - WINDOW), 0.0, NEG)
    kv = {False: (k2s_ref[pl.ds(kstart, span), :], v2s_ref[pl.ds(kstart, span), :]),
          True: (k2_ref[pl.ds(kstart, span), :], v2_ref[pl.ds(kstart, span), :])}
    grp = N_HEADS // B_KV_HEADS
    ss, ps, dens, outs = {}, {}, {}, {}

    def scores(h):
        p, e = divmod(h, 2)
        qp = q_ref[:, p * LANES:(p + 1) * LANES].astype(F32) * (HEAD_DIM ** -0.5)
        qm = jnp.where(lo_half if e == 0 else ~lo_half, qp, 0.0).astype(BF16)
        ss[h] = _dot_nt(qm, kv[(h // grp) == e][0]) + bias

    def softmax(h):
        s = ss.pop(h)
        sink = sinks_ref[layer, h]
        m = jnp.maximum(jnp.max(s, axis=1, keepdims=True), sink)
        pe = jnp.exp(s - m)
        dens[h] = jnp.sum(pe, axis=1, keepdims=True) + jnp.exp(sink - m)
        ps[h] = pe.astype(BF16)

    def values(h):
        vv = kv[(h // grp) == (h % 2)][1]
        outs[h] = jnp.dot(ps.pop(h), vv, preferred_element_type=F32) / dens.pop(h)
        if h % 2 == 1:
            _gated_store(o_ref, g_ref, h // 2, jnp.where(lo_half, outs.pop(h - 1), outs.pop(h)))

    stages = (scores, softmax, values)
    for step in range(N_HEADS + len(stages) - 1):
        for s_i, stage in enumerate(stages):
            if 0 <= step - s_i < N_HEADS:
                stage(step - s_i)


def _swa_call(zp, zr, sinks, layer, batch, seq, tq):
    nq = seq // tq
    kv_spec = lambda c: pl.BlockSpec((seq, LANES), lambda b, i: (b, (c - NP_PLAIN) // LANES))
    return pl.pallas_call(
        functools.partial(_swa_kernel, tq=tq, layer=layer),
        out_shape=jax.ShapeDtypeStruct((batch * seq, MIX_W), BF16),
        grid=(batch, nq),
        in_specs=[
            pl.BlockSpec(memory_space=pltpu.SMEM),
            pl.BlockSpec((tq, MIX_W), lambda b, i: (b * nq + i, (C_BQ - NP_PLAIN) // MIX_W)),
            kv_spec(C_BK2), kv_spec(C_BK2S), kv_spec(C_BV2), kv_spec(C_BV2S),
            pl.BlockSpec((tq, MIX_W), lambda b, i: (b * nq + i, C_BG // MIX_W)),
        ],
        out_specs=pl.BlockSpec((tq, MIX_W), lambda b, i: (b * nq + i, 0)),
        compiler_params=_cparams(2),
        name="swa_sinks_attention",
    )(sinks, zr, zr, zr, zr, zr, zp)


def _sb_kernel(q_ref, k_ref, v_ref, g_ref, o_ref, qm_sc, run_sc, acc_sc, *, tq, ck):
    i = pl.program_id(1)
    ndiag = tq // ck
    nck = (i + 1) * ndiag
    rep = ck // LANES
    lo_half = _lo_half(tq)
    row = i * tq + lax.broadcasted_iota(I32, (tq, ck), 0)
    col_l = lax.broadcasted_iota(I32, (tq, ck), 1)
    j_idx = lax.broadcasted_iota(I32, (2 * ck, ck), 0)
    j_idx = jnp.where(j_idx >= ck, j_idx - ck, j_idx)
    neg_later2 = jnp.where(j_idx > lax.broadcasted_iota(I32, (2 * ck, ck), 1),
                           -1.0, 0.0).astype(BF16)

    for p in range(N_PAIRS):
        qp = q_ref[:, p * LANES:(p + 1) * LANES].astype(F32) * (HEAD_DIM ** -0.5)
        qm_sc[2 * p] = jnp.where(lo_half, qp, 0.0).astype(BF16)
        qm_sc[2 * p + 1] = jnp.where(lo_half, 0.0, qp).astype(BF16)
    run_sc[...] = jnp.zeros(run_sc.shape, F32)
    acc_sc[...] = jnp.zeros(acc_sc.shape, F32)

    def chunk(t, masked):
        k0 = pl.multiple_of((nck - 1 - t) * ck, ck)
        strict = (k0 + col_l < row) if masked else None
        zs, nlms, lbs, afters = {}, {}, {}, {}

        def scores(h):
            kp = k_ref[pl.ds(k0, ck), (h // 2) * LANES:(h // 2 + 1) * LANES]
            zs[h] = _dot_nt(qm_sc[h], kp) * LOG2E

        def logs(h):
            z2 = zs.pop(h)
            nlm = jnp.maximum(z2, 0.0) + jnp.log2(1.0 + jnp.exp2(-jnp.abs(z2)))
            lbs[h] = z2 - nlm
            nlms[h] = jnp.where(strict, nlm, 0.0) if masked else nlm

        def cumsum(h):
            hi = nlms[h].astype(BF16)
            lo = (nlms[h] - hi.astype(F32)).astype(BF16)
            afters[h] = jnp.dot(jnp.concatenate([hi, lo], axis=1), neg_later2,
                                preferred_element_type=F32)

        def weights(h):
            run = run_sc[h]
            a = jnp.exp2(lbs.pop(h) + afters.pop(h) + jnp.tile(run, (1, rep)))
            if masked:
                a = jnp.where(strict, a, 0.0)
            vp = v_ref[pl.ds(k0, ck), (h // 2) * LANES:(h // 2 + 1) * LANES]
            acc_sc[h] += jnp.dot(a.astype(BF16), vp, preferred_element_type=F32)
            run_sc[h] = run - jnp.sum(nlms.pop(h), axis=1, keepdims=True)

        stages = (scores, logs, cumsum, weights)
        for step in range(N_HEADS + len(stages) - 1):
            for s, stage in enumerate(stages):
                if 0 <= step - s < N_HEADS:
                    stage(step - s)

    for t in range(ndiag):
        chunk(t, True)

    def live():
        return jnp.max(run_sc[...]) > RUN_FLOOR

    def body(carry):
        t, _ = carry
        chunk(t, False)
        return t + 1, live()

    lax.while_loop(lambda carry: jnp.logical_and(carry[0] < nck, carry[1]), body,
                   (jnp.int32(ndiag), live()))

    for p in range(N_PAIRS):
        _gated_store(o_ref, g_ref, p, jnp.where(lo_half, acc_sc[2 * p], acc_sc[2 * p + 1]))


def _sb_call(z, batch, seq, tq, ck):
    nq = seq // tq
    state = pltpu.VMEM((N_HEADS, tq, LANES), F32)
    return pl.pallas_call(
        functools.partial(_sb_kernel, tq=tq, ck=ck),
        out_shape=jax.ShapeDtypeStruct((batch * seq, MIX_W), BF16),
        grid=(batch, nq),
        in_specs=[
            pl.BlockSpec((tq, MIX_W), lambda b, i: (b * nq + i, C_CQ // MIX_W)),
            pl.BlockSpec((seq, MIX_W), lambda b, i: (b, C_CK // MIX_W)),
            pl.BlockSpec((seq, MIX_W), lambda b, i: (b, C_CV // MIX_W)),
            pl.BlockSpec((tq, MIX_W), lambda b, i: (b * nq + i, C_CG // MIX_W)),
        ],
        out_specs=pl.BlockSpec((tq, MIX_W), lambda b, i: (b * nq + i, 0)),
        scratch_shapes=[pltpu.VMEM((N_HEADS, tq, LANES), BF16), state, state],
        compiler_params=_cparams(2),
        name="stick_breaking_attention",
    )(z, z, z, z)


def _merge_kernel(ya_ref, yb_ref, yc_ref, ma_ref, mb_ref, mc_ref, x_ref, gate_ref,
                  wa_ref, wb_ref, wc_ref, wo_ref, fg_ref, o_ref, *, final):
    def branch(y_ref, m_ref, w_ref):
        pr = jnp.dot(y_ref[...], w_ref[0], preferred_element_type=F32)
        return _sigmoid(m_ref[...].astype(F32)) * pr

    merged = (branch(ya_ref, ma_ref, wa_ref) + branch(yb_ref, mb_ref, wb_ref)
              + branch(yc_ref, mc_ref, wc_ref))
    out = x_ref[...] + gate_ref[0] * jnp.dot(merged.astype(BF16), wo_ref[0],
                                             preferred_element_type=F32)
    if final:
        out = out * lax.rsqrt(jnp.mean(out * out, axis=-1, keepdims=True) + EPS) * fg_ref[...]
    o_ref[...] = out


def _merge_call(ya, yb, yc, z, h, mod3, wa, wb, wc, wo, fg, l, seq, tm, final):
    rows, d = h.shape
    nqb = seq // tm
    y_spec = pl.BlockSpec((tm, MIX_W), lambda i: (i, 0))
    m_spec = lambda c: pl.BlockSpec((tm, d), lambda i: (i, c // d))
    w_spec = pl.BlockSpec((1, MIX_W, d), lambda i: (l, 0, 0))
    return pl.pallas_call(
        functools.partial(_merge_kernel, final=final),
        out_shape=jax.ShapeDtypeStruct((rows, d), F32),
        grid=(rows // tm,),
        in_specs=[
            y_spec, y_spec, y_spec, m_spec(C_MA), m_spec(C_MB), m_spec(C_MC),
            pl.BlockSpec((tm, d), lambda i: (i, 0)),
            pl.BlockSpec((1, 1, d), lambda i: (l * MOD_ROWS + i // nqb, 0, 2)),
            w_spec, w_spec, w_spec,
            pl.BlockSpec((1, d, d), lambda i: (l, 0, 0)),
            pl.BlockSpec((1, d), lambda i: (0, 0)),
        ],
        out_specs=pl.BlockSpec((tm, d), lambda i: (i, 0)),
        compiler_params=_cparams(1),
        name="merge_outproj_residual",
    )(ya, yb, yc, z, z, z, h, mod3, wa, wb, wc, wo, fg.reshape(1, d))


def _tiles(seq):
    big = 512 if seq % 512 == 0 else 256
    return min(1024, seq), big, 256, big


def kernel(x, c, norm_g, w_ada, b_ada, w_in, sinks, w_br_a, w_br_b, w_br_c, w_out, final_g):
    batch, seq, d = x.shape
    depth = w_in.shape[0]
    assert d == D_MODEL and seq % 256 == 0 and seq >= 256 + WINDOW and batch <= MOD_ROWS
    tm_in, tq_a, tq, tm_mg = _tiles(seq)

    c_pad = jnp.zeros((MOD_ROWS, d), F32).at[:batch].set(c)
    mod3 = _mod_call(c_pad, w_ada, b_ada).reshape(depth * MOD_ROWS, 1, 3 * d)
    tabs = _rope_tables(seq)
    w_all = _prep_call(w_in)
    norm_g3 = norm_g.reshape(depth, 1, d)
    wa, wb, wc, wo = (w.astype(BF16) for w in (w_br_a, w_br_b, w_br_c, w_out))
    h = x.reshape(batch * seq, d)
    for l in range(depth):
        zp = _inproj_call(h, mod3, norm_g3, w_all, tabs, l, seq, tm_in, rope=False)
        zr = _inproj_call(h, mod3, norm_g3, w_all, tabs, l, seq, tm_in, rope=True)
        ya = _dsa_call(zp, zr, batch, seq, tq_a, tq_a)
        yb = _swa_call(zp, zr, sinks, l, batch, seq, tq)
        yc = _sb_call(zp, batch, seq, tq, tq)
        h = _merge_call(ya, yb, yc, zp, h, mod3, wa, wb, wc, wo, final_g, l, seq, tm_mg,
                        final=(l == depth - 1))
    return h.reshape(batch, seq, d)
```

```python
import functools

import jax
import jax.numpy as jnp
import numpy as np
from jax import lax
from jax.experimental import pallas as pl
from jax.experimental.pallas import tpu as pltpu

F32 = jnp.float32
BF16 = jnp.bfloat16
I32 = jnp.int32
I16 = jnp.int16

D_MODEL = 1024
HEAD_DIM = 64
ROPE_THETA = 10000.0
EPS = 1e-6
N_HEADS = 8
N_PAIRS = N_HEADS // 2
IDX_HEADS = 4
TOPK_MAX = 256
B_KV_HEADS = 2
WINDOW = 128
MIX_W = N_HEADS * HEAD_DIM

LANES = 128
MOD_ROWS = 8
VMEM_LIMIT = 56 * 1024 * 1024
NEG = -1e30
LOG2E = 1.4426950408889634
RUN_FLOOR = -160.0
INT_MIN = -(2 ** 31)
I16_MIN = -(2 ** 15)
SUB32, SUB16 = 8, 16

C_MA, C_MB, C_MC = 0, 1024, 2048
C_AV, C_AG, C_BG, C_CQ, C_CK, C_CV, C_CG = 3072, 3584, 4096, 4608, 5120, 5632, 6144
C_AQ, C_AK, C_BQ = 6656, 7168, 7680
C_IQ = 8192
C_IK2, C_BK2, C_BK2S, C_BV2, C_BV2S, C_IW = 8448, 8576, 8704, 8832, 8960, 9088
NP = 9216
TN = 512
TN_PLAIN = 1664
INPROJ_SPLIT = 4
NP_PLAIN = C_AQ
NP_ROPE = NP - NP_PLAIN
assert NP_PLAIN % TN_PLAIN == 0 and NP_PLAIN % TN == 0 and NP_ROPE % TN == 0 and C_BK2S == NP - TN

_SPLIT_NAMES = ("aq", "ak", "av", "ag", "iq", "ik", "iw", "bq", "bk", "bv", "bg",
                "cq", "ck", "cv", "cg", "ma", "mb", "mc")
_SPLIT_SIZES = (512, 512, 512, 512, 256, 64, 4, 512, 128, 128, 512,
                512, 512, 512, 512, 1024, 1024, 1024)


def _cparams(n_grid):
    return pltpu.CompilerParams(dimension_semantics=("arbitrary",) * n_grid,
                                vmem_limit_bytes=VMEM_LIMIT)


def _sigmoid(x):
    return 1.0 / (1.0 + jnp.exp(-x))


def _dot_nt(a, b):
    return lax.dot_general(a, b, (((1,), (1,)), ((), ())), preferred_element_type=F32)


def _lo_half(rows):
    return lax.broadcasted_iota(I32, (rows, LANES), 1) < HEAD_DIM


def _mod_kernel(c_ref, w_ref, b_ref, o_ref):
    c = c_ref[...]
    cs = c * _sigmoid(c)
    o_ref[0] = jnp.dot(cs, w_ref[0], preferred_element_type=F32,
                       precision=lax.Precision.HIGHEST) + b_ref[0]


def _mod_call(c_pad, w_ada, b_ada):
    depth, d, d3 = w_ada.shape
    rows = c_pad.shape[0]
    return pl.pallas_call(
        _mod_kernel,
        out_shape=jax.ShapeDtypeStruct((depth, rows, d3), F32),
        grid=(depth, d3 // d),
        in_specs=[
            pl.BlockSpec((rows, d), lambda l, j: (0, 0)),
            pl.BlockSpec((1, d, d), lambda l, j: (l, 0, j)),
            pl.BlockSpec((1, 1, d), lambda l, j: (l, 0, j)),
        ],
        out_specs=pl.BlockSpec((1, rows, d), lambda l, j: (l, 0, j)),
        compiler_params=_cparams(2),
        name="adaln_mod",
    )(c_pad, w_ada, b_ada.reshape(depth, 1, d3))


def _inproj_plain_kernel(x_ref, shift_ref, scale_ref, g_ref, w_ref, o_ref, u_ref):
    @pl.when(pl.program_id(1) == 0)
    def _():
        x = x_ref[...]
        y = x * lax.rsqrt(jnp.mean(x * x, axis=-1, keepdims=True) + EPS) * g_ref[0]
        u_ref[...] = (y * (1.0 + scale_ref[0]) + shift_ref[0]).astype(BF16)

    sub_rows = o_ref.shape[0] // INPROJ_SPLIT
    for r in range(INPROJ_SPLIT):
        rows = slice(r * sub_rows, (r + 1) * sub_rows)
        o_ref[rows, :] = jnp.dot(u_ref[rows, :], w_ref[0], preferred_element_type=F32).astype(BF16)


def _inproj_rope_kernel(u_ref, w_ref, tab_ref, o_ref):
    tm, tn = o_ref.shape
    sub_rows = tm // INPROJ_SPLIT
    lane = lax.broadcasted_iota(I32, (sub_rows, tn), 1)
    first = (lane & (HEAD_DIM - 1)) < HEAD_DIM // 2
    n_rope = jnp.where(pl.program_id(1) == pl.num_programs(1) - 1, LANES, tn)
    for r in range(INPROJ_SPLIT):
        rows = slice(r * sub_rows, (r + 1) * sub_rows)
        acc = jnp.dot(u_ref[rows, :], w_ref[0], preferred_element_type=F32)
        partner = jnp.where(first, pltpu.roll(acc, tn - HEAD_DIM // 2, 1),
                            pltpu.roll(acc, HEAD_DIM // 2, 1))
        roped = (acc * jnp.tile(tab_ref[0, rows, :], (1, tn // LANES))
                 + partner * jnp.tile(tab_ref[1, rows, :], (1, tn // LANES)))
        o_ref[rows, :] = jnp.where(lane < n_rope, roped, acc).astype(BF16)


def _inproj_calls(h, mod3, norm_g, w, tabs, l, seq, tm):
    rows, d = h.shape
    nqb = seq // tm
    zp, u = pl.pallas_call(
        _inproj_plain_kernel,
        out_shape=(jax.ShapeDtypeStruct((rows, NP_PLAIN), BF16), jax.ShapeDtypeStruct((rows, d), BF16)),
        grid=(rows // tm, NP_PLAIN // TN_PLAIN),
        in_specs=[
            pl.BlockSpec((tm, d), lambda i, j: (i, 0)),
            pl.BlockSpec((1, 1, d), lambda i, j: (l * MOD_ROWS + i // nqb, 0, 0)),
            pl.BlockSpec((1, 1, d), lambda i, j: (l * MOD_ROWS + i // nqb, 0, 1)),
            pl.BlockSpec((1, 1, d), lambda i, j: (l, 0, 0)),
            pl.BlockSpec((1, d, TN_PLAIN), lambda i, j: (l, 0, j)),
        ],
        out_specs=(pl.BlockSpec((tm, TN_PLAIN), lambda i, j: (i, j)),
                   pl.BlockSpec((tm, d), lambda i, j: (i, 0))),
        compiler_params=_cparams(2),
        name="norm_inproj",
    )(h, mod3, mod3, norm_g, w)
    zr = pl.pallas_call(
        _inproj_rope_kernel,
        out_shape=jax.ShapeDtypeStruct((rows, NP_ROPE), BF16),
        grid=(rows // tm, NP_ROPE // TN),
        in_specs=[
            pl.BlockSpec((tm, d), lambda i, j: (i, 0)),
            pl.BlockSpec((1, d, TN), lambda i, j: (l, 0, NP_PLAIN // TN + j)),
            pl.BlockSpec((2, tm, LANES), lambda i, j: (0, i % nqb, 0)),
        ],
        out_specs=pl.BlockSpec((tm, TN), lambda i, j: (i, j)),
        compiler_params=_cparams(2),
        name="inproj_rope",
    )(u, w, tabs)
    return zp, zr


def _rope_tables(seq):
    half = HEAD_DIM // 2
    inv = 1.0 / (ROPE_THETA ** (jnp.arange(0, HEAD_DIM, 2, dtype=F32) / HEAD_DIM))
    ang = jnp.arange(seq, dtype=F32)[:, None] * inv[None, :]
    lane = np.arange(LANES)
    sign = np.where((lane % HEAD_DIM) < half, -1.0, 1.0).astype(np.float32)
    return jnp.stack([jnp.cos(ang)[:, lane % half], jnp.sin(ang)[:, lane % half] * sign[None, :]])


_SRC = dict(zip(_SPLIT_NAMES, np.cumsum((0,) + _SPLIT_SIZES[:-1]).tolist()))
_WHOLE = ((C_MA, "ma"), (C_MB, "mb"), (C_MC, "mc"), (C_AV, "av"), (C_AG, "ag"), (C_BG, "bg"),
          (C_CQ, "cq"), (C_CK, "ck"), (C_CV, "cv"), (C_CG, "cg"), (C_AQ, "aq"), (C_AK, "ak"),
          (C_BQ, "bq"), (C_IQ, "iq"), (C_BK2, "bk"), (C_BV2, "bv"))
_HALVES = ((C_IK2, _SRC["ik"], _SRC["ik"]),
           (C_BK2S, _SRC["bk"] + HEAD_DIM, _SRC["bk"]),
           (C_BV2S, _SRC["bv"] + HEAD_DIM, _SRC["bv"]))


def _prep_kernel(w_ref, o_ref):
    rows = w_ref.shape[1]
    lane = lax.broadcasted_iota(I32, (rows, LANES), 1)
    sizes = dict(zip(_SPLIT_NAMES, _SPLIT_SIZES))
    for dst, name in _WHOLE:
        o_ref[0, :, dst:dst + sizes[name]] = w_ref[0, :, _SRC[name]:_SRC[name] + sizes[name]].astype(BF16)
    for dst, lo, hi in _HALVES:
        low = w_ref[0, :, lo:lo + LANES]
        high = w_ref[0, :, hi - HEAD_DIM:hi - HEAD_DIM + LANES]
        o_ref[0, :, dst:dst + LANES] = jnp.where(lane < HEAD_DIM, low, high).astype(BF16)
    iw = w_ref[0, :, _SRC["iw"]:_SRC["iw"] + LANES]
    o_ref[0, :, C_IW:C_IW + LANES] = jnp.where(lane < sizes["iw"], iw, 0.0).astype(BF16)


def _prep_call(w):
    depth, d, n = w.shape
    rows = 128
    return pl.pallas_call(
        _prep_kernel,
        out_shape=jax.ShapeDtypeStruct((depth, d, NP), BF16),
        grid=(depth, d // rows),
        in_specs=[pl.BlockSpec((1, rows, n), lambda l, i: (l, i, 0))],
        out_specs=pl.BlockSpec((1, rows, NP), lambda l, i: (l, i, 0)),
        compiler_params=_cparams(2),
        name="inproj_weight_layout",
    )(w)


def _gated_store(o_ref, g_ref, p, val):
    sl = slice(p * LANES, (p + 1) * LANES)
    g = g_ref[:, sl].astype(F32)
    o_ref[:, sl] = (val * (g * _sigmoid(g))).astype(o_ref.dtype)


def _dsa_kernel(q_ref, k_ref, v_ref, g_ref, iq_ref, ik_ref, iw_ref, o_ref,
                s_sc, khi_sc, klo_sc, qm_sc, iqm_sc, w_sc, m_sc, l_sc, acc_sc, *, tq, ck, topk):
    i = pl.program_id(1)
    ndiag = tq // ck
    nck = (i + 1) * ndiag
    nfull = nck - ndiag
    lo_half = _lo_half(tq)
    kf = jnp.float32(topk)

    for p in range(N_PAIRS):
        qp = q_ref[:, p * LANES:(p + 1) * LANES].astype(F32) * (HEAD_DIM ** -0.5)
        qm_sc[2 * p] = jnp.where(lo_half, qp, 0.0).astype(BF16)
        qm_sc[2 * p + 1] = jnp.where(lo_half, 0.0, qp).astype(BF16)
    for p in range(IDX_HEADS // 2):
        t = iq_ref[:, p * LANES:(p + 1) * LANES].astype(F32)
        iqm_sc[2 * p] = jnp.where(lo_half, t, 0.0).astype(BF16)
        iqm_sc[2 * p + 1] = jnp.where(lo_half, 0.0, t).astype(BF16)
    w_sc[...] = iw_ref[...].astype(F32).T[:SUB32] * ((IDX_HEADS ** -0.5) * (HEAD_DIM ** -0.5))

    q_pos = i * tq + lax.broadcasted_iota(I32, (ck, tq), 1)
    k_off = lax.broadcasted_iota(I32, (ck, tq), 0)

    def sort_key(x):
        bits = pltpu.bitcast(x, I32)
        return bits ^ ((bits >> 31) & 0x7FFFFFFF)

    def score_chunks(cs, masked):
        k0s = [pl.multiple_of(c * ck, ck) for c in cs]
        ds = [[_dot_nt(ik_ref[pl.ds(k0, ck), :], iqm_sc[h]) for h in range(IDX_HEADS)] for k0 in k0s]
        for c, k0, d in zip(cs, k0s, ds):
            sc = w_sc[0:1, :] * jnp.maximum(d[0], 0.0)
            for h in range(1, IDX_HEADS):
                sc = sc + w_sc[h:h + 1, :] * jnp.maximum(d[h], 0.0)
            if masked:
                sc = jnp.where(k0 + k_off <= q_pos, sc, -jnp.inf)
            s_sc[c] = sc
            key = sort_key(sc)
            khi_sc[c] = (key >> 16).astype(I16)
            klo_sc[c] = (key ^ 0x8000).astype(I16)

    def score_pair(c2, carry):
        score_chunks([2 * c2, 2 * c2 + 1], False)
        return carry

    lax.fori_loop(0, nfull // 2, score_pair, 0)

    @pl.when(nfull % 2 == 1)
    def _():
        score_chunks([nfull - 1], False)

    score_chunks([nfull + t for t in range(ndiag)], True)

    npair = (nck + 1) // 2

    @pl.when(nck % 2 == 1)
    def _():
        khi_sc[nck] = jnp.full((ck, tq), I16_MIN, I16)
        klo_sc[nck] = jnp.full((ck, tq), I16_MIN, I16)

    def count16(ref, thr, strict=False):
        def body(c2, parts):
            new = []
            for j, part in enumerate(parts):
                x = ref[2 * c2 + j]
                ones = jnp.where((x > thr) if strict else (x >= thr), jnp.bfloat16(1), jnp.bfloat16(0))
                for r in range(ck // SUB16):
                    part = part + ones[r * SUB16:(r + 1) * SUB16]
                new.append(part)
            return tuple(new)

        zero = jnp.zeros((SUB16, tq), BF16)
        parts = lax.fori_loop(0, npair, body, (zero, zero))
        return jnp.sum(parts[0].astype(F32) + parts[1].astype(F32), axis=0, keepdims=True)

    def bisect16(ref, want):
        def step(it, res):
            cand = res + jnp.left_shift(jnp.int32(1), 15 - it)
            return jnp.where(count16(ref, cand.astype(I16)) >= want, cand, res)

        return lax.fori_loop(0, 16, step, jnp.full((1, tq), I16_MIN, I32))

    hi = bisect16(khi_sc, kf)
    hi16 = hi.astype(I16)
    want_lo = kf - count16(khi_sc, hi16, strict=True)

    def low_half(c, carry):
        klo_sc[c] = jnp.where(khi_sc[c] == hi16, klo_sc[c], jnp.int16(I16_MIN))
        return carry

    lax.fori_loop(0, nck, low_half, 0)
    lo = bisect16(klo_sc, want_lo)
    res = (hi << 16) | ((lo ^ 0x8000) & 0xFFFF)
    seed = pltpu.bitcast(res ^ ((res >> 31) & 0x7FFFFFFF), F32)

    few = (i * tq + lax.broadcasted_iota(I32, (1, tq), 1) + 1).astype(F32) < kf
    floor = jnp.float32(jnp.finfo(jnp.float32).min)

    reducers = {"sum": (jnp.add, jnp.sum), "max": (jnp.maximum, jnp.max), "min": (jnp.minimum, jnp.min)}

    def fold(fn, init, kind):
        pair, red = reducers[kind]

        def body(c, part):
            return pair(part, red(fn(s_sc[c]).reshape(ck // SUB32, SUB32, tq), axis=0))

        part = lax.fori_loop(0, nck, body, jnp.full((SUB32, tq), init, F32))
        return red(part, axis=0, keepdims=True)

    def count(pred):
        return fold(lambda x: jnp.where(pred(x), 1.0, 0.0), 0.0, "sum")

    def any_true(mask):
        return jnp.max(jnp.where(mask, 1.0, 0.0)) > 0.5

    def down_cond(carry):
        g, c_ge = carry
        return any_true(jnp.logical_and(c_ge < kf, jnp.logical_not(few)))

    def down_body(carry):
        g, c_ge = carry
        below = fold(lambda x: jnp.where(x < g, x, -jnp.inf), -jnp.inf, "max")
        g = jnp.where(jnp.logical_and(c_ge < kf, jnp.logical_not(few)), below, g)
        return g, count(lambda x: x >= g)

    g0 = jnp.where(few, floor, seed)
    g, _ = lax.while_loop(down_cond, down_body, (g0, count(lambda x: x >= g0)))

    def up_cond(carry):
        v, c_gt = carry
        return any_true(jnp.logical_and(c_gt >= kf, jnp.logical_not(few)))

    def up_body(carry):
        v, c_gt = carry
        above = fold(lambda x: jnp.where(x > v, x, jnp.inf), jnp.inf, "min")
        v = jnp.where(jnp.logical_and(c_gt >= kf, jnp.logical_not(few)), above, v)
        return v, count(lambda x: x > v)

    v0 = jnp.where(few, floor, fold(lambda x: jnp.where(x >= g, x, jnp.inf), jnp.inf, "min"))
    tau, _ = lax.while_loop(up_cond, up_body, (v0, count(lambda x: x > v0)))
    n_ge = count(lambda x: x >= tau)

    @pl.when(any_true(jnp.logical_and(n_ge > kf, jnp.logical_not(few))))
    def _():
        need = kf - count(lambda x: x > tau)
        incl = (lax.broadcasted_iota(I32, (ck, ck), 1)
                <= lax.broadcasted_iota(I32, (ck, ck), 0)).astype(F32).astype(BF16)

        def body(c, seen):
            x = s_sc[c]
            eq = jnp.where(x == tau, 1.0, 0.0)
            rank = jnp.dot(incl, eq.astype(BF16), preferred_element_type=F32) + seen
            drop = eq * jnp.where(rank > need, 1.0, 0.0)
            s_sc[c] = jnp.where(drop > 0.5, -jnp.inf, x)
            return seen + jnp.sum(eq, axis=0, keepdims=True)

        lax.fori_loop(0, nck, body, jnp.zeros((1, tq), F32))

    m_sc[...] = jnp.full(m_sc.shape, NEG, F32)
    l_sc[...] = jnp.zeros(l_sc.shape, F32)
    acc_sc[...] = jnp.zeros(acc_sc.shape, F32)
    ones_rows = jnp.ones((SUB16, ck), BF16)

    def attend_chunks(cs):
        k0s = [pl.multiple_of(c * ck, ck) for c in cs]
        biases = [jnp.where(s_sc[c] >= tau, 0.0, NEG) for c in cs]
        ss, ps, alphas, vts = {}, {}, {}, {}

        def scores(j, h):
            sl = slice((h // 2) * LANES, (h // 2 + 1) * LANES)
            ss[j, h] = _dot_nt(k_ref[pl.ds(k0s[j], ck), sl], qm_sc[h]) + biases[j]
            if h % 2 == 0:
                vts[j, h // 2] = v_ref[pl.ds(k0s[j], ck), sl].astype(F32).T.astype(BF16)

        def softmax(j, h):
            s = ss.pop((j, h))
            m_prev = m_sc[h:h + 1, :]
            m_next = jnp.maximum(m_prev, jnp.max(s, axis=0, keepdims=True))
            ps[j, h] = jnp.exp(s - m_next).astype(BF16)
            alphas[j, h] = jnp.exp(m_prev - m_next)
            m_sc[h:h + 1, :] = m_next

        def values(j, h):
            e = h % 2
            vt = jnp.concatenate([vts[j, h // 2][e * HEAD_DIM:(e + 1) * HEAD_DIM], ones_rows], axis=0)
            pv = jnp.dot(vt, ps.pop((j, h)), preferred_element_type=F32)
            alpha = alphas.pop((j, h))
            l_sc[h:h + 1, :] = alpha * l_sc[h:h + 1, :] + pv[HEAD_DIM:HEAD_DIM + 1, :]
            acc_sc[h] = alpha * acc_sc[h] + pv[:HEAD_DIM, :]

        work = [(j, h) for j in range(len(cs)) for h in range(N_HEADS)]
        stages = (scores, softmax, values)
        for step in range(len(work) + len(stages) - 1):
            for s, stage in enumerate(stages):
                if 0 <= step - s < len(work):
                    stage(*work[step - s])

    def attend_pair(c2, carry):
        attend_chunks([2 * c2, 2 * c2 + 1])
        return carry

    lax.fori_loop(0, nck // 2, attend_pair, 0)

    @pl.when(nck % 2 == 1)
    def _():
        attend_chunks([nck - 1])

    for p in range(N_PAIRS):
        he, ho = 2 * p, 2 * p + 1
        out_t = jnp.concatenate([acc_sc[he] / l_sc[he:he + 1, :],
                                 acc_sc[ho] / l_sc[ho:ho + 1, :]], axis=0)
        _gated_store(o_ref, g_ref, p, out_t.T)


def _dsa_call(zp, zr, batch, seq, tq, ck):
    topk = min(TOPK_MAX, seq // 4)
    nq = seq // tq
    assert seq // SUB16 <= 256
    n_even = 2 * pl.cdiv(seq // ck, 2)
    kern = functools.partial(_dsa_kernel, tq=tq, ck=ck, topk=topk)
    return pl.pallas_call(
        kern,
        out_shape=jax.ShapeDtypeStruct((batch * seq, MIX_W), BF16),
        grid=(batch, nq),
        in_specs=[
            pl.BlockSpec((tq, MIX_W), lambda b, i: (b * nq + i, (C_AQ - NP_PLAIN) // MIX_W)),
            pl.BlockSpec((seq, MIX_W), lambda b, i: (b, (C_AK - NP_PLAIN) // MIX_W)),
            pl.BlockSpec((seq, MIX_W), lambda b, i: (b, C_AV // MIX_W)),
            pl.BlockSpec((tq, MIX_W), lambda b, i: (b * nq + i, C_AG // MIX_W)),
            pl.BlockSpec((tq, 2 * LANES), lambda b, i: (b * nq + i, (C_IQ - NP_PLAIN) // (2 * LANES))),
            pl.BlockSpec((seq, LANES), lambda b, i: (b, (C_IK2 - NP_PLAIN) // LANES)),
            pl.BlockSpec((tq, LANES), lambda b, i: (b * nq + i, (C_IW - NP_PLAIN) // LANES)),
        ],
        out_specs=pl.BlockSpec((tq, MIX_W), lambda b, i: (b * nq + i, 0)),
        scratch_shapes=[
            pltpu.VMEM((seq // ck, ck, tq), F32),
            pltpu.VMEM((n_even, ck, tq), I16),
            pltpu.VMEM((n_even, ck, tq), I16),
            pltpu.VMEM((N_HEADS, tq, LANES), BF16),
            pltpu.VMEM((IDX_HEADS, tq, LANES), BF16),
            pltpu.VMEM((SUB32, tq), F32),
            pltpu.VMEM((N_HEADS, tq), F32),
            pltpu.VMEM((N_HEADS, tq), F32),
            pltpu.VMEM((N_HEADS, HEAD_DIM, tq), F32),
        ],
        compiler_params=_cparams(2),
        name="dsa_attention",
    )(zr, zr, zp, zp, zr, zr, zr)


def _swa_kernel(sinks_ref, q_ref, k2_ref, k2s_ref, v2_ref, v2s_ref, g_ref, o_ref, *, tq, layer):
    i = pl.program_id(1)
    span = tq + WINDOW
    kstart = pl.multiple_of(jnp.maximum(i * tq - WINDOW, 0), WINDOW)
    lo_half = _lo_half(tq)
    row = i * tq + lax.broadcasted_iota(I32, (tq, span), 0)
    col = kstart + lax.broadcasted_iota(I32, (tq, span), 1)
    bias = jnp.where((col <= row) & (col > row - WINDOW), 0.0, NEG)
    kv = {False: (k2s_ref[pl.ds(kstart, span), :], v2s_ref[pl.ds(kstart, span), :]),
          True: (k2_ref[pl.ds(kstart, span), :], v2_ref[pl.ds(kstart, span), :])}
    grp = N_HEADS // B_KV_HEADS
    ss, ps, dens, outs = {}, {}, {}, {}

    def scores(h):
        p, e = divmod(h, 2)
        qp = q_ref[:, p * LANES:(p + 1) * LANES].astype(F32) * (HEAD_DIM ** -0.5)
        qm = jnp.where(lo_half if e == 0 else ~lo_half, qp, 0.0).astype(BF16)
        ss[h] = _dot_nt(qm, kv[(h // grp) == e][0]) + bias

    def softmax(h):
        s = ss.pop(h)
        sink = sinks_ref[layer, h]
        m = jnp.maximum(jnp.max(s, axis=1, keepdims=True), sink)
        pe = jnp.exp(s - m)
        dens[h] = jnp.sum(pe, axis=1, keepdims=True) + jnp.exp(sink - m)
        ps[h] = pe.astype(BF16)

    def values(h):
        vv = kv[(h // grp) == (h % 2)][1]
        outs[h] = jnp.dot(ps.pop(h), vv, preferred_element_type=F32) / dens.pop(h)
        if h % 2 == 1:
            _gated_store(o_ref, g_ref, h // 2, jnp.where(lo_half, outs.pop(h - 1), outs.pop(h)))

    stages = (scores, softmax, values)
    for step in range(N_HEADS + len(stages) - 1):
        for s_i, stage in enumerate(stages):
            if 0 <= step - s_i < N_HEADS:
                stage(step - s_i)


def _swa_call(zp, zr, sinks, layer, batch, seq, tq):
    nq = seq // tq
    kv_spec = lambda c: pl.BlockSpec((seq, LANES), lambda b, i: (b, (c - NP_PLAIN) // LANES))
    return pl.pallas_call(
        functools.partial(_swa_kernel, tq=tq, layer=layer),
        out_shape=jax.ShapeDtypeStruct((batch * seq, MIX_W), BF16),
        grid=(batch, nq),
        in_specs=[
            pl.BlockSpec(memory_space=pltpu.SMEM),
            pl.BlockSpec((tq, MIX_W), lambda b, i: (b * nq + i, (C_BQ - NP_PLAIN) // MIX_W)),
            kv_spec(C_BK2), kv_spec(C_BK2S), kv_spec(C_BV2), kv_spec(C_BV2S),
            pl.BlockSpec((tq, MIX_W), lambda b, i: (b * nq + i, C_BG // MIX_W)),
        ],
        out_specs=pl.BlockSpec((tq, MIX_W), lambda b, i: (b * nq + i, 0)),
        compiler_params=_cparams(2),
        name="swa_sinks_attention",
    )(sinks, zr, zr, zr, zr, zr, zp)


def _sb_kernel(q_ref, k_ref, v_ref, g_ref, o_ref, qm_sc, run_sc, acc_sc, *, tq, ck):
    i = pl.program_id(1)
    ndiag = tq // ck
    nck = (i + 1) * ndiag
    rep = ck // LANES
    lo_half = _lo_half(tq)
    row = i * tq + lax.broadcasted_iota(I32, (tq, ck), 0)
    col_l = lax.broadcasted_iota(I32, (tq, ck), 1)
    j_idx = lax.broadcasted_iota(I32, (2 * ck, ck), 0)
    j_idx = jnp.where(j_idx >= ck, j_idx - ck, j_idx)
    neg_later2 = jnp.where(j_idx > lax.broadcasted_iota(I32, (2 * ck, ck), 1),
                           -1.0, 0.0).astype(BF16)

    for p in range(N_PAIRS):
        qp = q_ref[:, p * LANES:(p + 1) * LANES].astype(F32) * (HEAD_DIM ** -0.5)
        qm_sc[2 * p] = jnp.where(lo_half, qp, 0.0).astype(BF16)
        qm_sc[2 * p + 1] = jnp.where(lo_half, 0.0, qp).astype(BF16)
    run_sc[...] = jnp.zeros(run_sc.shape, F32)
    acc_sc[...] = jnp.zeros(acc_sc.shape, F32)

    def chunk(t, masked):
        k0 = pl.multiple_of((nck - 1 - t) * ck, ck)
        strict = (k0 + col_l < row) if masked else None
        zs, nlms, lbs, afters = {}, {}, {}, {}

        def scores(h):
            kp = k_ref[pl.ds(k0, ck), (h // 2) * LANES:(h // 2 + 1) * LANES]
            zs[h] = _dot_nt(qm_sc[h], kp) * LOG2E

        def logs(h):
            z2 = zs.pop(h)
            nlm = jnp.maximum(z2, 0.0) + jnp.log2(1.0 + jnp.exp2(-jnp.abs(z2)))
            lbs[h] = z2 - nlm
            nlms[h] = jnp.where(strict, nlm, 0.0) if masked else nlm

        def cumsum(h):
            hi = nlms[h].astype(BF16)
            lo = (nlms[h] - hi.astype(F32)).astype(BF16)
            afters[h] = jnp.dot(jnp.concatenate([hi, lo], axis=1), neg_later2,
                                preferred_element_type=F32)

        def weights(h):
            run = run_sc[h]
            a = jnp.exp2(lbs.pop(h) + afters.pop(h) + jnp.tile(run, (1, rep)))
            if masked:
                a = jnp.where(strict, a, 0.0)
            vp = v_ref[pl.ds(k0, ck), (h // 2) * LANES:(h // 2 + 1) * LANES]
            acc_sc[h] += jnp.dot(a.astype(BF16), vp, preferred_element_type=F32)
            run_sc[h] = run - jnp.sum(nlms.pop(h), axis=1, keepdims=True)

        stages = (scores, logs, cumsum, weights)
        for step in range(N_HEADS + len(stages) - 1):
            for s, stage in enumerate(stages):
                if 0 <= step - s < N_HEADS:
                    stage(step - s)

    for t in range(ndiag):
        chunk(t, True)

    def live():
        return jnp.max(run_sc[...]) > RUN_FLOOR

    def body(carry):
        t, _ = carry
        chunk(t, False)
        return t + 1, live()

    lax.while_loop(lambda carry: jnp.logical_and(carry[0] < nck, carry[1]), body,
                   (jnp.int32(ndiag), live()))

    for p in range(N_PAIRS):
        _gated_store(o_ref, g_ref, p, jnp.where(lo_half, acc_sc[2 * p], acc_sc[2 * p + 1]))


def _sb_call(z, batch, seq, tq, ck):
    nq = seq // tq
    state = pltpu.VMEM((N_HEADS, tq, LANES), F32)
    return pl.pallas_call(
        functools.partial(_sb_kernel, tq=tq, ck=ck),
        out_shape=jax.ShapeDtypeStruct((batch * seq, MIX_W), BF16),
        grid=(batch, nq),
        in_specs=[
            pl.BlockSpec((tq, MIX_W), lambda b, i: (b * nq + i, C_CQ // MIX_W)),
            pl.BlockSpec((seq, MIX_W), lambda b, i: (b, C_CK // MIX_W)),
            pl.BlockSpec((seq, MIX_W), lambda b, i: (b, C_CV // MIX_W)),
            pl.BlockSpec((tq, MIX_W), lambda b, i: (b * nq + i, C_CG // MIX_W)),
        ],
        out_specs=pl.BlockSpec((tq, MIX_W), lambda b, i: (b * nq + i, 0)),
        scratch_shapes=[pltpu.VMEM((N_HEADS, tq, LANES), BF16), state, state],
        compiler_params=_cparams(2),
        name="stick_breaking_attention",
    )(z, z, z, z)


def _merge_kernel(ya_ref, yb_ref, yc_ref, ma_ref, mb_ref, mc_ref, x_ref, gate_ref,
                  wa_ref, wb_ref, wc_ref, wo_ref, fg_ref, o_ref, *, final):
    def branch(y_ref, m_ref, w_ref):
        pr = jnp.dot(y_ref[...], w_ref[0], preferred_element_type=F32)
        return _sigmoid(m_ref[...].astype(F32)) * pr

    merged = (branch(ya_ref, ma_ref, wa_ref) + branch(yb_ref, mb_ref, wb_ref)
              + branch(yc_ref, mc_ref, wc_ref))
    out = x_ref[...] + gate_ref[0] * jnp.dot(merged.astype(BF16), wo_ref[0],
                                             preferred_element_type=F32)
    if final:
        out = out * lax.rsqrt(jnp.mean(out * out, axis=-1, keepdims=True) + EPS) * fg_ref[...]
    o_ref[...] = out


def _merge_call(ya, yb, yc, z, h, mod3, wa, wb, wc, wo, fg, l, seq, tm, final):
    rows, d = h.shape
    nqb = seq // tm
    y_spec = pl.BlockSpec((tm, MIX_W), lambda i: (i, 0))
    m_spec = lambda c: pl.BlockSpec((tm, d), lambda i: (i, c // d))
    w_spec = pl.BlockSpec((1, MIX_W, d), lambda i: (l, 0, 0))
    return pl.pallas_call(
        functools.partial(_merge_kernel, final=final),
        out_shape=jax.ShapeDtypeStruct((rows, d), F32),
        grid=(rows // tm,),
        in_specs=[
            y_spec, y_spec, y_spec, m_spec(C_MA), m_spec(C_MB), m_spec(C_MC),
            pl.BlockSpec((tm, d), lambda i: (i, 0)),
            pl.BlockSpec((1, 1, d), lambda i: (l * MOD_ROWS + i // nqb, 0, 2)),
            w_spec, w_spec, w_spec,
            pl.BlockSpec((1, d, d), lambda i: (l, 0, 0)),
            pl.BlockSpec((1, d), lambda i: (0, 0)),
        ],
        out_specs=pl.BlockSpec((tm, d), lambda i: (i, 0)),
        compiler_params=_cparams(1),
        name="merge_outproj_residual",
    )(ya, yb, yc, z, z, z, h, mod3, wa, wb, wc, wo, fg.reshape(1, d))


def _tiles(seq):
    big = 512 if seq % 512 == 0 else 256
    return min(1024, seq), big, 256, big


def kernel(x, c, norm_g, w_ada, b_ada, w_in, sinks, w_br_a, w_br_b, w_br_c, w_out, final_g):
    batch, seq, d = x.shape
    depth = w_in.shape[0]
    assert d == D_MODEL and seq % 256 == 0 and seq >= 256 + WINDOW and batch <= MOD_ROWS
    tm_in, tq_a, tq, tm_mg = _tiles(seq)

    c_pad = jnp.zeros((MOD_ROWS, d), F32).at[:batch].set(c)
    mod3 = _mod_call(c_pad, w_ada, b_ada).reshape(depth * MOD_ROWS, 1, 3 * d)
    tabs = _rope_tables(seq)
    w_all = _prep_call(w_in)
    norm_g3 = norm_g.reshape(depth, 1, d)
    wa, wb, wc, wo = (w.astype(BF16) for w in (w_br_a, w_br_b, w_br_c, w_out))
    h = x.reshape(batch * seq, d)
    for l in range(depth):
        zp, zr = _inproj_calls(h, mod3, norm_g3, w_all, tabs, l, seq, tm_in)
        ya = _dsa_call(zp, zr, batch, seq, tq_a, tq_a)
        yb = _swa_call(zp, zr, sinks, l, batch, seq, tq)
        yc = _sb_call(zp, batch, seq, tq, tq)
        h = _merge_call(ya, yb, yc, zp, h, mod3, wa, wb, wc, wo, final_g, l, seq, tm_mg,
                        final=(l == depth - 1))
    return h.reshape(batch, seq, d)
```

```python
import functools

import jax
import jax.numpy as jnp
import numpy as np
from jax import lax
from jax.experimental import pallas as pl
from jax.experimental.pallas import tpu as pltpu

F32 = jnp.float32
BF16 = jnp.bfloat16
I32 = jnp.int32
I16 = jnp.int16

D_MODEL = 1024
HEAD_DIM = 64
ROPE_THETA = 10000.0
EPS = 1e-6
N_HEADS = 8
N_PAIRS = N_HEADS // 2
IDX_HEADS = 4
TOPK_MAX = 256
B_KV_HEADS = 2
WINDOW = 128
MIX_W = N_HEADS * HEAD_DIM

LANES = 128
MOD_ROWS = 8
VMEM_LIMIT = 56 * 1024 * 1024
NEG = -1e30
LOG2E = 1.4426950408889634
RUN_FLOOR = -160.0
INT_MIN = -(2 ** 31)
I16_MIN = -(2 ** 15)
SUB32, SUB16 = 8, 16
SEED_LOW_STEPS = 8

C_MA, C_MB, C_MC = 0, 1024, 2048
C_AV, C_AG, C_BG, C_CQ, C_CK, C_CV, C_CG = 3072, 3584, 4096, 4608, 5120, 5632, 6144
C_AQ, C_AK, C_BQ = 6656, 7168, 7680
C_IQ = 8192
C_IK2, C_BK2, C_BK2S, C_BV2, C_BV2S, C_IW = 8448, 8576, 8704, 8832, 8960, 9088
NP = 9216
TN = 512
TN_PLAIN = 1664
INPROJ_SPLIT = 4
NP_PLAIN = C_AQ
NP_ROPE = NP - NP_PLAIN
assert NP_PLAIN % TN_PLAIN == 0 and NP_PLAIN % TN == 0 and NP_ROPE % TN == 0 and C_BK2S == NP - TN

_SPLIT_NAMES = ("aq", "ak", "av", "ag", "iq", "ik", "iw", "bq", "bk", "bv", "bg",
                "cq", "ck", "cv", "cg", "ma", "mb", "mc")
_SPLIT_SIZES = (512, 512, 512, 512, 256, 64, 4, 512, 128, 128, 512,
                512, 512, 512, 512, 1024, 1024, 1024)


def _cparams(n_grid):
    return pltpu.CompilerParams(dimension_semantics=("arbitrary",) * n_grid,
                                vmem_limit_bytes=VMEM_LIMIT)


def _sigmoid(x):
    return 1.0 / (1.0 + jnp.exp(-x))


def _dot_nt(a, b):
    return lax.dot_general(a, b, (((1,), (1,)), ((), ())), preferred_element_type=F32)


def _lo_half(rows):
    return lax.broadcasted_iota(I32, (rows, LANES), 1) < HEAD_DIM


def _mod_kernel(c_ref, w_ref, b_ref, o_ref):
    c = c_ref[...]
    cs = c * _sigmoid(c)
    o_ref[0] = jnp.dot(cs, w_ref[0], preferred_element_type=F32,
                       precision=lax.Precision.HIGHEST) + b_ref[0]


def _mod_call(c_pad, w_ada, b_ada):
    depth, d, d3 = w_ada.shape
    rows = c_pad.shape[0]
    return pl.pallas_call(
        _mod_kernel,
        out_shape=jax.ShapeDtypeStruct((depth, rows, d3), F32),
        grid=(depth, d3 // d),
        in_specs=[
            pl.BlockSpec((rows, d), lambda l, j: (0, 0)),
            pl.BlockSpec((1, d, d), lambda l, j: (l, 0, j)),
            pl.BlockSpec((1, 1, d), lambda l, j: (l, 0, j)),
        ],
        out_specs=pl.BlockSpec((1, rows, d), lambda l, j: (l, 0, j)),
        compiler_params=_cparams(2),
        name="adaln_mod",
    )(c_pad, w_ada, b_ada.reshape(depth, 1, d3))


def _inproj_plain_kernel(x_ref, shift_ref, scale_ref, g_ref, w_ref, o_ref, u_ref):
    @pl.when(pl.program_id(1) == 0)
    def _():
        x = x_ref[...]
        y = x * lax.rsqrt(jnp.mean(x * x, axis=-1, keepdims=True) + EPS) * g_ref[0]
        u_ref[...] = (y * (1.0 + scale_ref[0]) + shift_ref[0]).astype(BF16)

    sub_rows = o_ref.shape[0] // INPROJ_SPLIT
    for r in range(INPROJ_SPLIT):
        rows = slice(r * sub_rows, (r + 1) * sub_rows)
        o_ref[rows, :] = jnp.dot(u_ref[rows, :], w_ref[0], preferred_element_type=F32).astype(BF16)


def _inproj_rope_kernel(u_ref, w_ref, tab_ref, o_ref):
    tm, tn = o_ref.shape
    sub_rows = tm // INPROJ_SPLIT
    lane = lax.broadcasted_iota(I32, (sub_rows, tn), 1)
    first = (lane & (HEAD_DIM - 1)) < HEAD_DIM // 2
    n_rope = jnp.where(pl.program_id(1) == pl.num_programs(1) - 1, LANES, tn)
    for r in range(INPROJ_SPLIT):
        rows = slice(r * sub_rows, (r + 1) * sub_rows)
        acc = jnp.dot(u_ref[rows, :], w_ref[0], preferred_element_type=F32)
        partner = jnp.where(first, pltpu.roll(acc, tn - HEAD_DIM // 2, 1),
                            pltpu.roll(acc, HEAD_DIM // 2, 1))
        roped = (acc * jnp.tile(tab_ref[0, rows, :], (1, tn // LANES))
                 + partner * jnp.tile(tab_ref[1, rows, :], (1, tn // LANES)))
        o_ref[rows, :] = jnp.where(lane < n_rope, roped, acc).astype(BF16)


def _inproj_calls(h, mod3, norm_g, w, tabs, l, seq, tm):
    rows, d = h.shape
    nqb = seq // tm
    zp, u = pl.pallas_call(
        _inproj_plain_kernel,
        out_shape=(jax.ShapeDtypeStruct((rows, NP_PLAIN), BF16), jax.ShapeDtypeStruct((rows, d), BF16)),
        grid=(rows // tm, NP_PLAIN // TN_PLAIN),
        in_specs=[
            pl.BlockSpec((tm, d), lambda i, j: (i, 0)),
            pl.BlockSpec((1, 1, d), lambda i, j: (l * MOD_ROWS + i // nqb, 0, 0)),
            pl.BlockSpec((1, 1, d), lambda i, j: (l * MOD_ROWS + i // nqb, 0, 1)),
            pl.BlockSpec((1, 1, d), lambda i, j: (l, 0, 0)),
            pl.BlockSpec((1, d, TN_PLAIN), lambda i, j: (l, 0, j)),
        ],
        out_specs=(pl.BlockSpec((tm, TN_PLAIN), lambda i, j: (i, j)),
                   pl.BlockSpec((tm, d), lambda i, j: (i, 0))),
        compiler_params=_cparams(2),
        name="norm_inproj",
    )(h, mod3, mod3, norm_g, w)
    zr = pl.pallas_call(
        _inproj_rope_kernel,
        out_shape=jax.ShapeDtypeStruct((rows, NP_ROPE), BF16),
        grid=(rows // tm, NP_ROPE // TN),
        in_specs=[
            pl.BlockSpec((tm, d), lambda i, j: (i, 0)),
            pl.BlockSpec((1, d, TN), lambda i, j: (l, 0, NP_PLAIN // TN + j)),
            pl.BlockSpec((2, tm, LANES), lambda i, j: (0, i % nqb, 0)),
        ],
        out_specs=pl.BlockSpec((tm, TN), lambda i, j: (i, j)),
        compiler_params=_cparams(2),
        name="inproj_rope",
    )(u, w, tabs)
    return zp, zr


def _rope_tables(seq):
    half = HEAD_DIM // 2
    inv = 1.0 / (ROPE_THETA ** (jnp.arange(0, HEAD_DIM, 2, dtype=F32) / HEAD_DIM))
    ang = jnp.arange(seq, dtype=F32)[:, None] * inv[None, :]
    lane = np.arange(LANES)
    sign = np.where((lane % HEAD_DIM) < half, -1.0, 1.0).astype(np.float32)
    return jnp.stack([jnp.cos(ang)[:, lane % half], jnp.sin(ang)[:, lane % half] * sign[None, :]])


_SRC = dict(zip(_SPLIT_NAMES, np.cumsum((0,) + _SPLIT_SIZES[:-1]).tolist()))
_WHOLE = ((C_MA, "ma"), (C_MB, "mb"), (C_MC, "mc"), (C_AV, "av"), (C_AG, "ag"), (C_BG, "bg"),
          (C_CQ, "cq"), (C_CK, "ck"), (C_CV, "cv"), (C_CG, "cg"), (C_AQ, "aq"), (C_AK, "ak"),
          (C_BQ, "bq"), (C_IQ, "iq"), (C_BK2, "bk"), (C_BV2, "bv"))
_HALVES = ((C_IK2, _SRC["ik"], _SRC["ik"]),
           (C_BK2S, _SRC["bk"] + HEAD_DIM, _SRC["bk"]),
           (C_BV2S, _SRC["bv"] + HEAD_DIM, _SRC["bv"]))


def _prep_kernel(w_ref, o_ref):
    rows = w_ref.shape[1]
    lane = lax.broadcasted_iota(I32, (rows, LANES), 1)
    sizes = dict(zip(_SPLIT_NAMES, _SPLIT_SIZES))
    for dst, name in _WHOLE:
        o_ref[0, :, dst:dst + sizes[name]] = w_ref[0, :, _SRC[name]:_SRC[name] + sizes[name]].astype(BF16)
    for dst, lo, hi in _HALVES:
        low = w_ref[0, :, lo:lo + LANES]
        high = w_ref[0, :, hi - HEAD_DIM:hi - HEAD_DIM + LANES]
        o_ref[0, :, dst:dst + LANES] = jnp.where(lane < HEAD_DIM, low, high).astype(BF16)
    iw = w_ref[0, :, _SRC["iw"]:_SRC["iw"] + LANES]
    o_ref[0, :, C_IW:C_IW + LANES] = jnp.where(lane < sizes["iw"], iw, 0.0).astype(BF16)


def _prep_call(w):
    depth, d, n = w.shape
    rows = 128
    return pl.pallas_call(
        _prep_kernel,
        out_shape=jax.ShapeDtypeStruct((depth, d, NP), BF16),
        grid=(depth, d // rows),
        in_specs=[pl.BlockSpec((1, rows, n), lambda l, i: (l, i, 0))],
        out_specs=pl.BlockSpec((1, rows, NP), lambda l, i: (l, i, 0)),
        compiler_params=_cparams(2),
        name="inproj_weight_layout",
    )(w)


def _gated_store(o_ref, g_ref, p, val):
    sl = slice(p * LANES, (p + 1) * LANES)
    g = g_ref[:, sl].astype(F32)
    o_ref[:, sl] = (val * (g * _sigmoid(g))).astype(o_ref.dtype)


def _dsa_kernel(q_ref, k_ref, v_ref, g_ref, iq_ref, ik_ref, iw_ref, o_ref,
                s_sc, khi_sc, klo_sc, qm_sc, iqm_sc, w_sc, m_sc, l_sc, acc_sc, *, tq, ck, topk):
    i = pl.program_id(1)
    ndiag = tq // ck
    nck = (i + 1) * ndiag
    nfull = nck - ndiag
    lo_half = _lo_half(tq)
    kf = jnp.float32(topk)

    for p in range(N_PAIRS):
        qp = q_ref[:, p * LANES:(p + 1) * LANES].astype(F32) * (HEAD_DIM ** -0.5)
        qm_sc[2 * p] = jnp.where(lo_half, qp, 0.0).astype(BF16)
        qm_sc[2 * p + 1] = jnp.where(lo_half, 0.0, qp).astype(BF16)
    for p in range(IDX_HEADS // 2):
        t = iq_ref[:, p * LANES:(p + 1) * LANES].astype(F32)
        iqm_sc[2 * p] = jnp.where(lo_half, t, 0.0).astype(BF16)
        iqm_sc[2 * p + 1] = jnp.where(lo_half, 0.0, t).astype(BF16)
    w_sc[...] = iw_ref[...].astype(F32).T[:SUB32] * ((IDX_HEADS ** -0.5) * (HEAD_DIM ** -0.5))

    q_pos = i * tq + lax.broadcasted_iota(I32, (ck, tq), 1)
    k_off = lax.broadcasted_iota(I32, (ck, tq), 0)

    def sort_key(x):
        bits = pltpu.bitcast(x, I32)
        return bits ^ ((bits >> 31) & 0x7FFFFFFF)

    def score_chunks(cs, masked):
        k0s = [pl.multiple_of(c * ck, ck) for c in cs]
        ds = [[_dot_nt(ik_ref[pl.ds(k0, ck), :], iqm_sc[h]) for h in range(IDX_HEADS)] for k0 in k0s]
        for c, k0, d in zip(cs, k0s, ds):
            sc = w_sc[0:1, :] * jnp.maximum(d[0], 0.0)
            for h in range(1, IDX_HEADS):
                sc = sc + w_sc[h:h + 1, :] * jnp.maximum(d[h], 0.0)
            if masked:
                sc = jnp.where(k0 + k_off <= q_pos, sc, -jnp.inf)
            s_sc[c] = sc
            key = sort_key(sc)
            khi_sc[c] = (key >> 16).astype(I16)
            klo_sc[c] = (key ^ 0x8000).astype(I16)

    def score_pair(c2, carry):
        score_chunks([2 * c2, 2 * c2 + 1], False)
        return carry

    lax.fori_loop(0, nfull // 2, score_pair, 0)

    @pl.when(nfull % 2 == 1)
    def _():
        score_chunks([nfull - 1], False)

    score_chunks([nfull + t for t in range(ndiag)], True)

    npair = (nck + 1) // 2

    @pl.when(nck % 2 == 1)
    def _():
        khi_sc[nck] = jnp.full((ck, tq), I16_MIN, I16)
        klo_sc[nck] = jnp.full((ck, tq), I16_MIN, I16)

    def count16(ref, thr, strict=False):
        def body(c2, parts):
            new = []
            for j, part in enumerate(parts):
                x = ref[2 * c2 + j]
                ones = jnp.where((x > thr) if strict else (x >= thr), jnp.bfloat16(1), jnp.bfloat16(0))
                for r in range(ck // SUB16):
                    part = part + ones[r * SUB16:(r + 1) * SUB16]
                new.append(part)
            return tuple(new)

        zero = jnp.zeros((SUB16, tq), BF16)
        parts = lax.fori_loop(0, npair, body, (zero, zero))
        return jnp.sum(parts[0].astype(F32) + parts[1].astype(F32), axis=0, keepdims=True)

    def bisect16(ref, want, steps=16):
        def step(it, res):
            cand = res + jnp.left_shift(jnp.int32(1), 15 - it)
            return jnp.where(count16(ref, cand.astype(I16)) >= want, cand, res)

        return lax.fori_loop(0, steps, step, jnp.full((1, tq), I16_MIN, I32))

    hi = bisect16(khi_sc, kf)
    hi16 = hi.astype(I16)
    want_lo = kf - count16(khi_sc, hi16, strict=True)

    def low_half(c, carry):
        klo_sc[c] = jnp.where(khi_sc[c] == hi16, klo_sc[c], jnp.int16(I16_MIN))
        return carry

    lax.fori_loop(0, nck, low_half, 0)
    lo = bisect16(klo_sc, want_lo, SEED_LOW_STEPS)
    res = (hi << 16) | ((lo ^ 0x8000) & 0xFFFF)
    seed = pltpu.bitcast(res ^ ((res >> 31) & 0x7FFFFFFF), F32)

    few = (i * tq + lax.broadcasted_iota(I32, (1, tq), 1) + 1).astype(F32) < kf
    floor = jnp.float32(jnp.finfo(jnp.float32).min)

    reducers = {"sum": (jnp.add, jnp.sum), "max": (jnp.maximum, jnp.max), "min": (jnp.minimum, jnp.min)}

    def fold(fn, init, kind):
        pair, red = reducers[kind]

        def body(c, part):
            return pair(part, red(fn(s_sc[c]).reshape(ck // SUB32, SUB32, tq), axis=0))

        part = lax.fori_loop(0, nck, body, jnp.full((SUB32, tq), init, F32))
        return red(part, axis=0, keepdims=True)

    def count(pred):
        return fold(lambda x: jnp.where(pred(x), 1.0, 0.0), 0.0, "sum")

    def any_true(mask):
        return jnp.max(jnp.where(mask, 1.0, 0.0)) > 0.5

    def down_cond(carry):
        g, c_ge = carry
        return any_true(jnp.logical_and(c_ge < kf, jnp.logical_not(few)))

    def down_body(carry):
        g, c_ge = carry
        below = fold(lambda x: jnp.where(x < g, x, -jnp.inf), -jnp.inf, "max")
        g = jnp.where(jnp.logical_and(c_ge < kf, jnp.logical_not(few)), below, g)
        return g, count(lambda x: x >= g)

    g0 = jnp.where(few, floor, seed)
    g, _ = lax.while_loop(down_cond, down_body, (g0, count(lambda x: x >= g0)))

    def up_cond(carry):
        v, c_gt = carry
        return any_true(jnp.logical_and(c_gt >= kf, jnp.logical_not(few)))

    def up_body(carry):
        v, c_gt = carry
        above = fold(lambda x: jnp.where(x > v, x, jnp.inf), jnp.inf, "min")
        v = jnp.where(jnp.logical_and(c_gt >= kf, jnp.logical_not(few)), above, v)
        return v, count(lambda x: x > v)

    v0 = jnp.where(few, floor, fold(lambda x: jnp.where(x >= g, x, jnp.inf), jnp.inf, "min"))
    tau, _ = lax.while_loop(up_cond, up_body, (v0, count(lambda x: x > v0)))
    n_ge = count(lambda x: x >= tau)

    @pl.when(any_true(jnp.logical_and(n_ge > kf, jnp.logical_not(few))))
    def _():
        need = kf - count(lambda x: x > tau)
        incl = (lax.broadcasted_iota(I32, (ck, ck), 1)
                <= lax.broadcasted_iota(I32, (ck, ck), 0)).astype(F32).astype(BF16)

        def body(c, seen):
            x = s_sc[c]
            eq = jnp.where(x == tau, 1.0, 0.0)
            rank = jnp.dot(incl, eq.astype(BF16), preferred_element_type=F32) + seen
            drop = eq * jnp.where(rank > need, 1.0, 0.0)
            s_sc[c] = jnp.where(drop > 0.5, -jnp.inf, x)
            return seen + jnp.sum(eq, axis=0, keepdims=True)

        lax.fori_loop(0, nck, body, jnp.zeros((1, tq), F32))

    m_sc[...] = jnp.full(m_sc.shape, NEG, F32)
    l_sc[...] = jnp.zeros(l_sc.shape, F32)
    acc_sc[...] = jnp.zeros(acc_sc.shape, F32)
    ones_rows = jnp.ones((SUB16, ck), BF16)

    def attend_chunks(cs):
        k0s = [pl.multiple_of(c * ck, ck) for c in cs]
        biases = [jnp.where(s_sc[c] >= tau, 0.0, NEG) for c in cs]
        ss, ps, alphas, vts = {}, {}, {}, {}

        def scores(j, h):
            sl = slice((h // 2) * LANES, (h // 2 + 1) * LANES)
            ss[j, h] = _dot_nt(k_ref[pl.ds(k0s[j], ck), sl], qm_sc[h]) + biases[j]
            if h % 2 == 0:
                vts[j, h // 2] = v_ref[pl.ds(k0s[j], ck), sl].astype(F32).T.astype(BF16)

        def softmax(j, h):
            s = ss.pop((j, h))
            m_prev = m_sc[h:h + 1, :]
            m_next = jnp.maximum(m_prev, jnp.max(s, axis=0, keepdims=True))
            ps[j, h] = jnp.exp(s - m_next).astype(BF16)
            alphas[j, h] = jnp.exp(m_prev - m_next)
            m_sc[h:h + 1, :] = m_next

        def values(j, h):
            e = h % 2
            vt = jnp.concatenate([vts[j, h // 2][e * HEAD_DIM:(e + 1) * HEAD_DIM], ones_rows], axis=0)
            pv = jnp.dot(vt, ps.pop((j, h)), preferred_element_type=F32)
            alpha = alphas.pop((j, h))
            l_sc[h:h + 1, :] = alpha * l_sc[h:h + 1, :] + pv[HEAD_DIM:HEAD_DIM + 1, :]
            acc_sc[h] = alpha * acc_sc[h] + pv[:HEAD_DIM, :]

        work = [(j, h) for j in range(len(cs)) for h in range(N_HEADS)]
        stages = (scores, softmax, values)
        for step in range(len(work) + len(stages) - 1):
            for s, stage in enumerate(stages):
                if 0 <= step - s < len(work):
                    stage(*work[step - s])

    def attend_pair(c2, carry):
        attend_chunks([2 * c2, 2 * c2 + 1])
        return carry

    lax.fori_loop(0, nck // 2, attend_pair, 0)

    @pl.when(nck % 2 == 1)
    def _():
        attend_chunks([nck - 1])

    for p in range(N_PAIRS):
        he, ho = 2 * p, 2 * p + 1
        out_t = jnp.concatenate([acc_sc[he] / l_sc[he:he + 1, :],
                                 acc_sc[ho] / l_sc[ho:ho + 1, :]], axis=0)
        _gated_store(o_ref, g_ref, p, out_t.T)


def _dsa_call(zp, zr, batch, seq, tq, ck):
    topk = min(TOPK_MAX, seq // 4)
    nq = seq // tq
    assert seq // SUB16 <= 256
    n_even = 2 * pl.cdiv(seq // ck, 2)
    kern = functools.partial(_dsa_kernel, tq=tq, ck=ck, topk=topk)
    return pl.pallas_call(
        kern,
        out_shape=jax.ShapeDtypeStruct((batch * seq, MIX_W), BF16),
        grid=(batch, nq),
        in_specs=[
            pl.BlockSpec((tq, MIX_W), lambda b, i: (b * nq + i, (C_AQ - NP_PLAIN) // MIX_W)),
            pl.BlockSpec((seq, MIX_W), lambda b, i: (b, (C_AK - NP_PLAIN) // MIX_W)),
            pl.BlockSpec((seq, MIX_W), lambda b, i: (b, C_AV // MIX_W)),
            pl.BlockSpec((tq, MIX_W), lambda b, i: (b * nq + i, C_AG // MIX_W)),
            pl.BlockSpec((tq, 2 * LANES), lambda b, i: (b * nq + i, (C_IQ - NP_PLAIN) // (2 * LANES))),
            pl.BlockSpec((seq, LANES), lambda b, i: (b, (C_IK2 - NP_PLAIN) // LANES)),
            pl.BlockSpec((tq, LANES), lambda b, i: (b * nq + i, (C_IW - NP_PLAIN) // LANES)),
        ],
        out_specs=pl.BlockSpec((tq, MIX_W), lambda b, i: (b * nq + i, 0)),
        scratch_shapes=[
            pltpu.VMEM((seq // ck, ck, tq), F32),
            pltpu.VMEM((n_even, ck, tq), I16),
            pltpu.VMEM((n_even, ck, tq), I16),
            pltpu.VMEM((N_HEADS, tq, LANES), BF16),
            pltpu.VMEM((IDX_HEADS, tq, LANES), BF16),
            pltpu.VMEM((SUB32, tq), F32),
            pltpu.VMEM((N_HEADS, tq), F32),
            pltpu.VMEM((N_HEADS, tq), F32),
            pltpu.VMEM((N_HEADS, HEAD_DIM, tq), F32),
        ],
        compiler_params=_cparams(2),
        name="dsa_attention",
    )(zr, zr, zp, zp, zr, zr, zr)


def _swa_kernel(sinks_ref, q_ref, k2_ref, k2s_ref, v2_ref, v2s_ref, g_ref, o_ref, *, tq, layer):
    i = pl.program_id(1)
    span = tq + WINDOW
    kstart = pl.multiple_of(jnp.maximum(i * tq - WINDOW, 0), WINDOW)
    lo_half = _lo_half(tq)
    row = i * tq + lax.broadcasted_iota(I32, (tq, span), 0)
    col = kstart + lax.broadcasted_iota(I32, (tq, span), 1)
    bias = jnp.where((col <= row) & (col > row - WINDOW), 0.0, NEG)
    kv = {False: (k2s_ref[pl.ds(kstart, span), :], v2s_ref[pl.ds(kstart, span), :]),
          True: (k2_ref[pl.ds(kstart, span), :], v2_ref[pl.ds(kstart, span), :])}
    grp = N_HEADS // B_KV_HEADS
    ss, ps, dens, outs = {}, {}, {}, {}

    def scores(h):
        p, e = divmod(h, 2)
        qp = q_ref[:, p * LANES:(p + 1) * LANES].astype(F32) * (HEAD_DIM ** -0.5)
        qm = jnp.where(lo_half if e == 0 else ~lo_half, qp, 0.0).astype(BF16)
        ss[h] = _dot_nt(qm, kv[(h // grp) == e][0]) + bias

    def softmax(h):
        s = ss.pop(h)
        sink = sinks_ref[layer, h]
        m = jnp.maximum(jnp.max(s, axis=1, keepdims=True), sink)
        pe = jnp.exp(s - m)
        dens[h] = jnp.sum(pe, axis=1, keepdims=True) + jnp.exp(sink - m)
        ps[h] = pe.astype(BF16)

    def values(h):
        vv = kv[(h // grp) == (h % 2)][1]
        outs[h] = jnp.dot(ps.pop(h), vv, preferred_element_type=F32) / dens.pop(h)
        if h % 2 == 1:
            _gated_store(o_ref, g_ref, h // 2, jnp.where(lo_half, outs.pop(h - 1), outs.pop(h)))

    stages = (scores, softmax, values)
    for step in range(N_HEADS + len(stages) - 1):
        for s_i, stage in enumerate(stages):
            if 0 <= step - s_i < N_HEADS:
                stage(step - s_i)


def _swa_call(zp, zr, sinks, layer, batch, seq, tq):
    nq = seq // tq
    kv_spec = lambda c: pl.BlockSpec((seq, LANES), lambda b, i: (b, (c - NP_PLAIN) // LANES))
    return pl.pallas_call(
        functools.partial(_swa_kernel, tq=tq, layer=layer),
        out_shape=jax.ShapeDtypeStruct((batch * seq, MIX_W), BF16),
        grid=(batch, nq),
        in_specs=[
            pl.BlockSpec(memory_space=pltpu.SMEM),
            pl.BlockSpec((tq, MIX_W), lambda b, i: (b * nq + i, (C_BQ - NP_PLAIN) // MIX_W)),
            kv_spec(C_BK2), kv_spec(C_BK2S), kv_spec(C_BV2), kv_spec(C_BV2S),
            pl.BlockSpec((tq, MIX_W), lambda b, i: (b * nq + i, C_BG // MIX_W)),
        ],
        out_specs=pl.BlockSpec((tq, MIX_W), lambda b, i: (b * nq + i, 0)),
        compiler_params=_cparams(2),
        name="swa_sinks_attention",
    )(sinks, zr, zr, zr, zr, zr, zp)


def _sb_kernel(q_ref, k_ref, v_ref, g_ref, o_ref, qm_sc, run_sc, acc_sc, *, tq, ck):
    i = pl.program_id(1)
    ndiag = tq // ck
    nck = (i + 1) * ndiag
    rep = ck // LANES
    lo_half = _lo_half(tq)
    row = i * tq + lax.broadcasted_iota(I32, (tq, ck), 0)
    col_l = lax.broadcasted_iota(I32, (tq, ck), 1)
    j_idx = lax.broadcasted_iota(I32, (2 * ck, ck), 0)
    j_idx = jnp.where(j_idx >= ck, j_idx - ck, j_idx)
    neg_later2 = jnp.where(j_idx > lax.broadcasted_iota(I32, (2 * ck, ck), 1),
                           -1.0, 0.0).astype(BF16)

    for p in range(N_PAIRS):
        qp = q_ref[:, p * LANES:(p + 1) * LANES].astype(F32) * (HEAD_DIM ** -0.5)
        qm_sc[2 * p] = jnp.where(lo_half, qp, 0.0).astype(BF16)
        qm_sc[2 * p + 1] = jnp.where(lo_half, 0.0, qp).astype(BF16)
    run_sc[...] = jnp.zeros(run_sc.shape, F32)
    acc_sc[...] = jnp.zeros(acc_sc.shape, F32)

    def chunk(t, masked):
        k0 = pl.multiple_of((nck - 1 - t) * ck, ck)
        strict = (k0 + col_l < row) if masked else None
        zs, nlms, lbs, afters = {}, {}, {}, {}

        def scores(h):
            kp = k_ref[pl.ds(k0, ck), (h // 2) * LANES:(h // 2 + 1) * LANES]
            zs[h] = _dot_nt(qm_sc[h], kp) * LOG2E

        def logs(h):
            z2 = zs.pop(h)
            nlm = jnp.maximum(z2, 0.0) + jnp.log2(1.0 + jnp.exp2(-jnp.abs(z2)))
            lbs[h] = z2 - nlm
            nlms[h] = jnp.where(strict, nlm, 0.0) if masked else nlm

        def cumsum(h):
            hi = nlms[h].astype(BF16)
            lo = (nlms[h] - hi.astype(F32)).astype(BF16)
            afters[h] = jnp.dot(jnp.concatenate([hi, lo], axis=1), neg_later2,
                                preferred_element_type=F32)

        def weights(h):
            run = run_sc[h]
            a = jnp.exp2(lbs.pop(h) + afters.pop(h) + jnp.tile(run, (1, rep)))
            if masked:
                a = jnp.where(strict, a, 0.0)
            vp = v_ref[pl.ds(k0, ck), (h // 2) * LANES:(h // 2 + 1) * LANES]
            acc_sc[h] += jnp.dot(a.astype(BF16), vp, preferred_element_type=F32)
            run_sc[h] = run - jnp.sum(nlms.pop(h), axis=1, keepdims=True)

        stages = (scores, logs, cumsum, weights)
        for step in range(N_HEADS + len(stages) - 1):
            for s, stage in enumerate(stages):
                if 0 <= step - s < N_HEADS:
                    stage(step - s)

    for t in range(ndiag):
        chunk(t, True)

    def live():
        return jnp.max(run_sc[...]) > RUN_FLOOR

    def body(carry):
        t, _ = carry
        chunk(t, False)
        return t + 1, live()

    lax.while_loop(lambda carry: jnp.logical_and(carry[0] < nck, carry[1]), body,
                   (jnp.int32(ndiag), live()))

    for p in range(N_PAIRS):
        _gated_store(o_ref, g_ref, p, jnp.where(lo_half, acc_sc[2 * p], acc_sc[2 * p + 1]))


def _sb_call(z, batch, seq, tq, ck):
    nq = seq // tq
    state = pltpu.VMEM((N_HEADS, tq, LANES), F32)
    return pl.pallas_call(
        functools.partial(_sb_kernel, tq=tq, ck=ck),
        out_shape=jax.ShapeDtypeStruct((batch * seq, MIX_W), BF16),
        grid=(batch, nq),
        in_specs=[
            pl.BlockSpec((tq, MIX_W), lambda b, i: (b * nq + i, C_CQ // MIX_W)),
            pl.BlockSpec((seq, MIX_W), lambda b, i: (b, C_CK // MIX_W)),
            pl.BlockSpec((seq, MIX_W), lambda b, i: (b, C_CV // MIX_W)),
            pl.BlockSpec((tq, MIX_W), lambda b, i: (b * nq + i, C_CG // MIX_W)),
        ],
        out_specs=pl.BlockSpec((tq, MIX_W), lambda b, i: (b * nq + i, 0)),
        scratch_shapes=[pltpu.VMEM((N_HEADS, tq, LANES), BF16), state, state],
        compiler_params=_cparams(2),
        name="stick_breaking_attention",
    )(z, z, z, z)


def _merge_kernel(ya_ref, yb_ref, yc_ref, ma_ref, mb_ref, mc_ref, x_ref, gate_ref,
                  wa_ref, wb_ref, wc_ref, wo_ref, fg_ref, o_ref, *, final):
    def branch(y_ref, m_ref, w_ref):
        pr = jnp.dot(y_ref[...], w_ref[0], preferred_element_type=F32)
        return _sigmoid(m_ref[...].astype(F32)) * pr

    merged = (branch(ya_ref, ma_ref, wa_ref) + branch(yb_ref, mb_ref, wb_ref)
              + branch(yc_ref, mc_ref, wc_ref))
    out = x_ref[...] + gate_ref[0] * jnp.dot(merged.astype(BF16), wo_ref[0],
                                             preferred_element_type=F32)
    if final:
        out = out * lax.rsqrt(jnp.mean(out * out, axis=-1, keepdims=True) + EPS) * fg_ref[...]
    o_ref[...] = out


def _merge_call(ya, yb, yc, z, h, mod3, wa, wb, wc, wo, fg, l, seq, tm, final):
    rows, d = h.shape
    nqb = seq // tm
    y_spec = pl.BlockSpec((tm, MIX_W), lambda i: (i, 0))
    m_spec = lambda c: pl.BlockSpec((tm, d), lambda i: (i, c // d))
    w_spec = pl.BlockSpec((1, MIX_W, d), lambda i: (l, 0, 0))
    return pl.pallas_call(
        functools.partial(_merge_kernel, final=final),
        out_shape=jax.ShapeDtypeStruct((rows, d), F32),
        grid=(rows // tm,),
        in_specs=[
            y_spec, y_spec, y_spec, m_spec(C_MA), m_spec(C_MB), m_spec(C_MC),
            pl.BlockSpec((tm, d), lambda i: (i, 0)),
            pl.BlockSpec((1, 1, d), lambda i: (l * MOD_ROWS + i // nqb, 0, 2)),
            w_spec, w_spec, w_spec,
            pl.BlockSpec((1, d, d), lambda i: (l, 0, 0)),
            pl.BlockSpec((1, d), lambda i: (0, 0)),
        ],
        out_specs=pl.BlockSpec((tm, d), lambda i: (i, 0)),
        compiler_params=_cparams(1),
        name="merge_outproj_residual",
    )(ya, yb, yc, z, z, z, h, mod3, wa, wb, wc, wo, fg.reshape(1, d))


def _tiles(seq):
    big = 512 if seq % 512 == 0 else 256
    return min(1024, seq), big, 256, big


def kernel(x, c, norm_g, w_ada, b_ada, w_in, sinks, w_br_a, w_br_b, w_br_c, w_out, final_g):
    batch, seq, d = x.shape
    depth = w_in.shape[0]
    assert d == D_MODEL and seq % 256 == 0 and seq >= 256 + WINDOW and batch <= MOD_ROWS
    tm_in, tq_a, tq, tm_mg = _tiles(seq)

    c_pad = jnp.zeros((MOD_ROWS, d), F32).at[:batch].set(c)
    mod3 = _mod_call(c_pad, w_ada, b_ada).reshape(depth * MOD_ROWS, 1, 3 * d)
    tabs = _rope_tables(seq)
    w_all = _prep_call(w_in)
    norm_g3 = norm_g.reshape(depth, 1, d)
    wa, wb, wc, wo = (w.astype(BF16) for w in (w_br_a, w_br_b, w_br_c, w_out))
    h = x.reshape(batch * seq, d)
    for l in range(depth):
        zp, zr = _inproj_calls(h, mod3, norm_g3, w_all, tabs, l, seq, tm_in)
        ya = _dsa_call(zp, zr, batch, seq, tq_a, tq_a)
        yb = _swa_call(zp, zr, sinks, l, batch, seq, tq)
        yc = _sb_call(zp, batch, seq, tq, tq)
        h = _merge_call(ya, yb, yc, zp, h, mod3, wa, wb, wc, wo, final_g, l, seq, tm_mg,
                        final=(l == depth - 1))
    return h.reshape(batch, seq, d)
```
